```python
import math
import jax, jax.numpy as jnp
from jax import lax
import numpy as np

D_MODEL = 1024
BATCH = 32
SEQ = 2048
DEPTH = 4

GRID_W = 64
CTX_LEN = 256
N_EVEN = (DEPTH + 1) // 2
N_ODD = DEPTH // 2
EPS = 1e-6
ROPE_BASE = 10000.0

RET_HEADS = 4
RET_DK = 128
RET_DV = 128
RET_QK_W = RET_HEADS * RET_DK
RET_W = RET_HEADS * RET_DV
RET_CHUNK = 128

POOL_GROUPS = 4
POOL_GC = 128
POOL_W = POOL_GROUPS * POOL_GC
POOL_WINDOWS = (2, 4, 8, 16)

EVEN_IN = 2 * RET_QK_W + 2 * RET_W + 2 * POOL_W
EVEN_OUT = RET_W + POOL_W

DIFF_HEADS = 8
DIFF_DH = 64
DIFF_DV = 2 * DIFF_DH
DIFF_QK_W = DIFF_HEADS * 2 * DIFF_DH
DIFF_W = DIFF_HEADS * DIFF_DV
ODD_IN = 2 * DIFF_QK_W + 2 * DIFF_W
Q_BLOCK = 128

kernel_name = "hybrid_retention_pool_diffattn_prefix_dit"


def rmsnorm(x, g):
    xf = x.astype(jnp.float32)
    y = xf * lax.rsqrt(jnp.mean(xf * xf, axis=-1, keepdims=True) + EPS)
    return (y * g.astype(jnp.float32)).astype(x.dtype)


def head_layernorm(x):
    xf = x.astype(jnp.float32)
    mu = jnp.mean(xf, axis=-1, keepdims=True)
    var = jnp.mean(jnp.square(xf - mu), axis=-1, keepdims=True)
    return ((xf - mu) * lax.rsqrt(var + EPS)).astype(x.dtype)


def _rot_half_axis(xa, pos):
    da = xa.shape[-1]
    freqs = ROPE_BASE ** (-jnp.arange(da // 2, dtype=jnp.float32) * 2.0 / da)
    ang = pos[:, None] * freqs[None, :]
    cos = jnp.cos(ang).astype(xa.dtype)
    sin = jnp.sin(ang).astype(xa.dtype)
    x1, x2 = xa[..., : da // 2], xa[..., da // 2:]
    return jnp.concatenate([x1 * cos - x2 * sin, x1 * sin + x2 * cos], axis=-1)


def axial_rope(x, rows, cols):
    half = x.shape[-1] // 2
    return jnp.concatenate([_rot_half_axis(x[..., :half], rows), _rot_half_axis(x[..., half:], cols)], axis=-1)


def to_heads(t, n_heads):
    b, s, _ = t.shape
    return t.reshape(b, s, n_heads, -1).transpose(0, 2, 1, 3)


def from_heads(t):
    b, h, s, d = t.shape
    return t.transpose(0, 2, 1, 3).reshape(b, s, h * d)


def retention_scan(q, k, v, log_g, s0, strict, reverse):
    if reverse:
        q, k, v = jnp.flip(q, axis=2), jnp.flip(k, axis=2), jnp.flip(v, axis=2)
    b, h, t, dk = q.shape
    dv = v.shape[-1]
    n = t // RET_CHUNK
    idx = jnp.arange(RET_CHUNK, dtype=jnp.float32)
    rel = idx[:, None] - idx[None, :]
    mask = (rel > 0) if strict else (rel >= 0)
    lg = log_g.astype(jnp.float32)
    intra = jnp.where(mask[None], jnp.exp(lg[:, None, None] * jnp.where(mask, rel, 0.0)[None]), 0.0).astype(q.dtype)
    q_dec = jnp.exp(lg[:, None] * (idx + 1.0))[..., None].astype(q.dtype)
    k_dec = jnp.exp(lg[:, None] * (RET_CHUNK - 1.0 - idx))[..., None].astype(q.dtype)
    c_dec = jnp.exp(lg * RET_CHUNK)[:, None, None].astype(q.dtype)

    def chunks(a):
        return jnp.moveaxis(a.reshape(b, h, n, RET_CHUNK, a.shape[-1]), 2, 0)

    def step(s, inp):
        qc, kc, vc = inp
        sc = jnp.einsum('bhid,bhjd->bhij', qc, kc) * intra
        o = jnp.einsum('bhij,bhjv->bhiv', sc, vc) + jnp.einsum('bhid,bhdv->bhiv', qc * q_dec, s)
        s = s * c_dec + jnp.einsum('bhjd,bhjv->bhdv', kc * k_dec, vc)
        return s, o

    s, o = lax.scan(step, s0, (chunks(q), chunks(k), chunks(v)))
    o = jnp.moveaxis(o, 0, 2).reshape(b, h, t, dv)
    if reverse:
        o = jnp.flip(o, axis=2)
    return o, s


def pool_mix(p, pool_w, pool_scale):
    b, t, _ = p.shape
    csum = jnp.concatenate([jnp.zeros((b, 1, POOL_W), jnp.float32), jnp.cumsum(p.astype(jnp.float32), axis=1)], axis=1)
    pos = jnp.arange(t)
    outs = []
    for g, win in enumerate(POOL_WINDOWS):
        lo = jnp.maximum(pos - win // 2, 0)
        hi = jnp.minimum(pos + win - win // 2 - 1, t - 1)
        cg = csum[..., g * POOL_GC:(g + 1) * POOL_GC]
        mean = (cg[:, hi + 1] - cg[:, lo]) / (hi - lo + 1).astype(jnp.float32)[None, :, None]
        outs.append(mean.astype(p.dtype) - p[..., g * POOL_GC:(g + 1) * POOL_GC])
    d = jnp.stack(outs, axis=2)
    y = jnp.einsum('btgc,gcd->btgd', d, pool_w).reshape(b, t, POOL_W)
    return y * pool_scale


def even_mixer(h, hc, w_in, w_out, decay_logit, pool_w, pool_scale, rows, cols, ctx_out):
    cut = [RET_QK_W, 2 * RET_QK_W, 2 * RET_QK_W + RET_W, 2 * RET_QK_W + 2 * RET_W, 2 * RET_QK_W + 2 * RET_W + POOL_W]
    q, k, v, g_r, p, g_p = jnp.split(h @ w_in, cut, axis=-1)
    qc, kc, vc, g_rc, pc, g_pc = jnp.split(hc @ w_in, cut, axis=-1)
    kscale = RET_DK ** -0.5
    q = axial_rope(to_heads(q, RET_HEADS), rows, cols)
    k = axial_rope(to_heads(k, RET_HEADS), rows, cols) * kscale
    v = to_heads(v, RET_HEADS)
    qc, kc, vc = to_heads(qc, RET_HEADS), to_heads(kc, RET_HEADS) * kscale, to_heads(vc, RET_HEADS)
    log_g = -jnp.exp(decay_logit.astype(jnp.float32))
    b = h.shape[0]
    s_zero = jnp.zeros((b, RET_HEADS, RET_DK, RET_DV), q.dtype)
    oc_f, sc_f = retention_scan(qc, kc, vc, log_g[0], s_zero, False, False)
    oc_b, sc_b = retention_scan(qc, kc, vc, log_g[1], s_zero, True, True)
    o_f, _ = retention_scan(q, k, v, log_g[0], sc_f, False, False)
    o_b, _ = retention_scan(q, k, v, log_g[1], sc_b, True, True)
    ret = from_heads(head_layernorm(o_f + o_b)) * jax.nn.silu(g_r)
    pool = pool_mix(p, pool_w, pool_scale) * jax.nn.silu(g_p)
    y = jnp.concatenate([ret, pool], axis=-1) @ w_out
    if not ctx_out:
        return y, None
    ret_c = from_heads(head_layernorm(oc_f + oc_b)) * jax.nn.silu(g_rc)
    pool_c = pool_mix(pc, pool_w, pool_scale) * jax.nn.silu(g_pc)
    yc = jnp.concatenate([ret_c, pool_c], axis=-1) @ w_out
    return y, yc


def diff_attend(q, k, v, lam):
    s = jnp.einsum('bhmqd,bhmkd->bhmqk', q, k).astype(jnp.float32) * (DIFF_DH ** -0.5)
    p = jax.nn.softmax(s, axis=-1)
    a = (p[:, :, 0] - lam * p[:, :, 1]).astype(v.dtype)
    return jnp.einsum('bhqk,bhkv->bhqv', a, v)


def diff_attend_blocks(q, k, v, lam):
    b, h, m, t, dh = q.shape
    nb = t // Q_BLOCK
    qb = jnp.moveaxis(q.reshape(b, h, m, nb, Q_BLOCK, dh), 3, 0)
    o = lax.map(lambda qi: diff_attend(qi, k, v, lam), qb)
    return jnp.moveaxis(o, 0, 2).reshape(b, h, t, v.shape[-1])


def odd_mixer(h, hc, w_in, w_out, lam_p, subln, layer_idx, rows, cols, ctx_out):
    cut = [DIFF_QK_W, 2 * DIFF_QK_W, 2 * DIFF_QK_W + DIFF_W]
    q, k, v, g = jnp.split(h @ w_in, cut, axis=-1)
    qc, kc, vc, gc = jnp.split(hc @ w_in, cut, axis=-1)

    def split_qk(t):
        b, s, _ = t.shape
        return t.reshape(b, s, DIFF_HEADS, 2, DIFF_DH).transpose(0, 2, 3, 1, 4)

    q = axial_rope(split_qk(q), rows, cols)
    k = axial_rope(split_qk(k), rows, cols)
    v = to_heads(v, DIFF_HEADS)
    qc, kc, vc = split_qk(qc), split_qk(kc), to_heads(vc, DIFF_HEADS)
    lam_init = 0.8 - 0.6 * math.exp(-0.3 * layer_idx)
    lp = lam_p.astype(jnp.float32)
    lam = jnp.exp(jnp.sum(lp[0] * lp[1])) - jnp.exp(jnp.sum(lp[2] * lp[3])) + lam_init
    k_all = jnp.concatenate([kc, k], axis=3)
    v_all = jnp.concatenate([vc, v], axis=2)
    o = diff_attend_blocks(q, k_all, v_all, lam)
    o = from_heads(rmsnorm(o, subln) * (1.0 - lam_init)) * jax.nn.silu(g)
    y = o @ w_out
    if not ctx_out:
        return y, None
    oc = diff_attend(qc, kc, vc, lam)
    oc = from_heads(rmsnorm(oc, subln) * (1.0 - lam_init)) * jax.nn.silu(gc)
    return y, oc @ w_out


def setup_inputs(seed: int = 0) -> dict:
    key = jax.random.key(seed)
    ks = jax.random.split(key, 18)
    f32 = jnp.float32
    d = D_MODEL
    base_decay = jnp.log(-jnp.log1p(-(2.0 ** (-5.0 - jnp.arange(RET_HEADS, dtype=f32)))))
    return {
        "x": jax.random.normal(ks[0], (BATCH, SEQ, d), f32),
        "c": jax.random.normal(ks[1], (BATCH, d), f32),
        "ctx": jax.random.normal(ks[2], (BATCH, CTX_LEN, d), f32),
        "c_ctx": jax.random.normal(ks[3], (d,), f32),
        "ada_w": jax.random.normal(ks[4], (DEPTH, d, 3 * d), f32) * (0.5 * d ** -0.5),
        "ada_b": jax.random.normal(ks[5], (DEPTH, 3 * d), f32) * 0.01,
        "norm_pre": 1.0 + 0.05 * jax.random.normal(ks[6], (DEPTH, d), f32),
        "norm_post": 1.0 + 0.05 * jax.random.normal(ks[7], (DEPTH, d), f32),
        "ev_w_in": jax.random.normal(ks[8], (N_EVEN, d, EVEN_IN), f32) * d ** -0.5,
        "ev_w_out": jax.random.normal(ks[9], (N_EVEN, EVEN_OUT, d), f32) * EVEN_OUT ** -0.5,
        "ret_decay_logit": base_decay[None, None, :] + 0.05 * jax.random.normal(ks[10], (N_EVEN, 2, RET_HEADS), f32),
        "pool_w": jax.random.normal(ks[11], (N_EVEN, POOL_GROUPS, POOL_GC, POOL_GC), f32) * POOL_GC ** -0.5,
        "pool_scale": 1.0 + 0.1 * jax.random.normal(ks[12], (N_EVEN, POOL_W), f32),
        "od_w_in": jax.random.normal(ks[13], (N_ODD, d, ODD_IN), f32) * d ** -0.5,
        "od_w_out": jax.random.normal(ks[14], (N_ODD, DIFF_W, d), f32) * DIFF_W ** -0.5,
        "diff_lambda": 0.1 * jax.random.normal(ks[15], (N_ODD, 4, DIFF_DH), f32),
        "diff_subln": 1.0 + 0.05 * jax.random.normal(ks[16], (N_ODD, DIFF_DV), f32),
    }


def reference(x, c, ctx, c_ctx, ada_w, ada_b, norm_pre, norm_post, ev_w_in, ev_w_out, ret_decay_logit,
              pool_w, pool_scale, od_w_in, od_w_out, diff_lambda, diff_subln):
    t = x.shape[1]
    ROWS = t // GRID_W
    rows = jnp.repeat(jnp.arange(ROWS), GRID_W).astype(jnp.float32)
    cols = jnp.tile(jnp.arange(GRID_W), ROWS).astype(jnp.float32)
    xc = ctx
    for l in range(DEPTH):
        ctx_out = l < DEPTH - 1
        mod = jax.nn.silu(c) @ ada_w[l] + ada_b[l]
        shift, scale, gate = jnp.split(mod[:, None, :], 3, axis=-1)
        mod_c = jax.nn.silu(c_ctx) @ ada_w[l] + ada_b[l]
        shift_c, scale_c, gate_c = jnp.split(mod_c, 3, axis=-1)
        h = rmsnorm(x, norm_pre[l]) * (1.0 + scale) + shift
        hc = rmsnorm(xc, norm_pre[l]) * (1.0 + scale_c) + shift_c
        i = l // 2
        if l % 2 == 0:
            y, yc = even_mixer(h, hc, ev_w_in[i], ev_w_out[i], ret_decay_logit[i], pool_w[i], pool_scale[i],
                               rows, cols, ctx_out)
        else:
            y, yc = odd_mixer(h, hc, od_w_in[i], od_w_out[i], diff_lambda[i], diff_subln[i], l,
                              rows, cols, ctx_out)
        x = x + gate * rmsnorm(y, norm_post[l])
        if ctx_out:
            xc = xc + gate_c * rmsnorm(yc, norm_post[l])
    return x
```

```python
import functools
import math

import jax
import jax.numpy as jnp
from jax import lax
from jax.experimental import pallas as pl
from jax.experimental.pallas import tpu as pltpu

F32 = jnp.float32
BF16 = jnp.bfloat16

D_MODEL = 1024
DEPTH = 4
GRID_W = 64
EPS = 1e-6
ROPE_BASE = 10000.0
LANES = 128

RET_HEADS = 4
RET_DK = 128
RET_CHUNK = 128
POOL_GROUPS = 4
POOL_WINDOWS = (2, 4, 8, 16)
POOL_PAD = 8
EVEN_IN = 3072
EVEN_NB = EVEN_IN // LANES

DIFF_HEADS = 8
DIFF_DH = 64
ODD_IN = 4096
ODD_NB = ODD_IN // LANES

TM = 256
PROJ_TN = 512
VMEM_LIMIT = 48 * 1024 * 1024


def _cparams(n_axes):
    return pltpu.CompilerParams(dimension_semantics=("parallel",) * n_axes, vmem_limit_bytes=VMEM_LIMIT)


def _silu(x):
    return x * jax.nn.sigmoid(x)


def _mod_kernel(c_ref, w_ref, b_ref, o_ref):
    a = _silu(c_ref[...])
    o_ref[0] = jnp.dot(a, w_ref[0], preferred_element_type=F32, precision=lax.Precision.HIGHEST) + b_ref[0]


def _modulation(c_aug, ada_w, ada_b):
    rows = c_aug.shape[0]
    nt = 3 * D_MODEL // D_MODEL
    return pl.pallas_call(
        _mod_kernel,
        out_shape=jax.ShapeDtypeStruct((DEPTH, rows, 3 * D_MODEL), F32),
        grid=(DEPTH, nt),
        in_specs=[
            pl.BlockSpec((rows, D_MODEL), lambda l, n: (0, 0)),
            pl.BlockSpec((1, D_MODEL, D_MODEL), lambda l, n: (l, 0, n)),
            pl.BlockSpec((1, 1, D_MODEL), lambda l, n: (l, 0, n)),
        ],
        out_specs=pl.BlockSpec((1, rows, D_MODEL), lambda l, n: (l, 0, n)),
        compiler_params=_cparams(2),
        name="adaln_modulation",
    )(c_aug, ada_w, ada_b.reshape(DEPTH, 1, 3 * D_MODEL))


def _lam_kernel(lp_ref, o_ref, *, lam_inits):
    for i, lam_init in enumerate(lam_inits):
        lp = lp_ref[i]
        a = jnp.sum(lp[0:1] * lp[1:2], axis=-1, keepdims=True)
        b = jnp.sum(lp[2:3] * lp[3:4], axis=-1, keepdims=True)
        lam = jnp.exp(a) - jnp.exp(b) + lam_init
        o_ref[i] = jnp.broadcast_to(lam, (1, LANES))


def _diff_lambdas(diff_lambda, lam_inits):
    n = diff_lambda.shape[0]
    return pl.pallas_call(
        functools.partial(_lam_kernel, lam_inits=lam_inits),
        out_shape=jax.ShapeDtypeStruct((n, 1, LANES), F32),
        name="diff_lambda",
    )(diff_lambda)


def _proj_kernel(x_ref, mod_ref, g_ref, w_ref, c_ref, s1_ref, s2_ref, o_ref, *, n_rope, n_q, rot, k_scale):
    x = x_ref[0]
    m = mod_ref[0, 0]
    y = x * lax.rsqrt(jnp.mean(x * x, axis=-1, keepdims=True) + EPS) * g_ref[...]
    h = (y * (1.0 + m[1:2]) + m[0:1]).astype(BF16)
    cos, sin_up, sin_dn = c_ref[...], s1_ref[...], s2_ref[...]
    per = PROJ_TN // LANES
    for n in range(w_ref.shape[1] // PROJ_TN):
        r = jnp.dot(h, w_ref[:, n * PROJ_TN:(n + 1) * PROJ_TN], preferred_element_type=F32)
        for cb in range(per):
            blk = n * per + cb
            v = r[:, cb * LANES:(cb + 1) * LANES]
            if blk < n_rope:
                v = v * cos + pltpu.roll(v, LANES - rot, 1) * sin_up + pltpu.roll(v, rot, 1) * sin_dn
                if blk >= n_q:
                    v = v * k_scale
            o_ref[0, blk] = v.astype(BF16)


def _project(xx, modsel, g_pre, w_bf16, tables, *, n_rope, n_q, rot, k_scale):
    b, l, d = xx.shape
    n_in = w_bf16.shape[1]
    nb = n_in // LANES
    kern = functools.partial(_proj_kernel, n_rope=n_rope, n_q=n_q, rot=rot, k_scale=k_scale)
    tab_spec = pl.BlockSpec((TM, LANES), lambda i, j: (j, 0))
    return pl.pallas_call(
        kern,
        out_shape=jax.ShapeDtypeStruct((b, nb, l, LANES), BF16),
        grid=(b, l // TM),
        in_specs=[
            pl.BlockSpec((1, TM, d), lambda i, j: (i, j, 0)),
            pl.BlockSpec((1, 1, 3, d), lambda i, j: (i, jnp.minimum(j, 1), 0, 0)),
            pl.BlockSpec((1, d), lambda i, j: (0, 0)),
            pl.BlockSpec((d, n_in), lambda i, j: (0, 0)),
            tab_spec, tab_spec, tab_spec,
        ],
        out_specs=pl.BlockSpec((1, nb, TM, LANES), lambda i, j: (i, 0, j, 0)),
        compiler_params=_cparams(2),
        name="input_projection",
    )(xx, modsel, g_pre, w_bf16, *tables)


def _out_kernel(za_ref, zb_ref, w_ref, x_ref, mod_ref, g_ref, o_ref):
    half = za_ref.shape[-1]
    y = jnp.dot(za_ref[0], w_ref[0:half], preferred_element_type=F32)
    y = y + jnp.dot(zb_ref[0], w_ref[half:2 * half], preferred_element_type=F32)
    r = y * lax.rsqrt(jnp.mean(y * y, axis=-1, keepdims=True) + EPS) * g_ref[...]
    o_ref[0] = x_ref[0] + mod_ref[0, 0][2:3] * r


def _out_project(za, zb, zb_col, w_bf16, xx, modsel, g_post, *, latent_only):
    b, l, d = xx.shape
    half = d // 2
    skip = 1 if latent_only else 0
    z_skip = skip - (l - za.shape[1]) // TM
    n_blocks = l // TM - skip
    out_rows = n_blocks * TM
    return pl.pallas_call(
        _out_kernel,
        out_shape=jax.ShapeDtypeStruct((b, out_rows, d), F32),
        grid=(b, n_blocks),
        in_specs=[
            pl.BlockSpec((1, TM, half), lambda i, j: (i, j + z_skip, 0)),
            pl.BlockSpec((1, TM, half), lambda i, j: (i, j + z_skip, zb_col)),
            pl.BlockSpec((d, d), lambda i, j: (0, 0)),
            pl.BlockSpec((1, TM, d), lambda i, j: (i, j + skip, 0)),
            pl.BlockSpec((1, 1, 3, d), lambda i, j: (i, jnp.minimum(j + skip, 1), 0, 0)),
            pl.BlockSpec((1, d), lambda i, j: (0, 0)),
        ],
        out_specs=pl.BlockSpec((1, TM, d), lambda i, j: (i, j, 0)),
        input_output_aliases={} if latent_only else {3: 0},
        compiler_params=_cparams(2),
        name="output_projection",
    )(za, zb, w_bf16, xx, modsel, g_post)


def _ret_kernel(dl_ref, q_ref, k_ref, v_ref, g_ref, o_ref, u_scr, s_scr, *, n_ctx_chunks):
    c = RET_CHUNK
    n_chunks = q_ref.shape[2] // c
    lg = -jnp.exp(dl_ref[0])
    lgf, lgb = lg[0:1], lg[1:2]
    ri = lax.broadcasted_iota(jnp.int32, (c, c), 0).astype(F32)
    ci = lax.broadcasted_iota(jnp.int32, (c, c), 1).astype(F32)
    rel = ri - ci
    decay = jnp.where(rel >= 0.0, jnp.exp(lgf * jnp.maximum(rel, 0.0)), jnp.exp(lgb * jnp.maximum(-rel, 0.0)))
    q_dec_f = jnp.exp(lgf * (ri + 1.0))
    q_dec_b = jnp.exp(lgb * (c - ri))
    k_dec_f = jnp.exp(lgf * (c - 1.0 - ri))
    k_dec_b = jnp.exp(lgb * ri)
    c_dec_f = jnp.exp(lgf * c)
    c_dec_b = jnp.exp(lgb * c)

    def chunk_rows(i):
        return pl.ds(pl.multiple_of(i * c, c), c)

    def kv_body(i, carry):
        kc = k_ref[0, 0, chunk_rows(i), :].astype(F32)
        vc = v_ref[0, 0, chunk_rows(i), :]
        kk = jnp.concatenate([kc * k_dec_f, kc * k_dec_b], axis=1).astype(BF16)
        u_scr[i] = lax.dot_general(kk, vc, (((0,), (0,)), ((), ())), preferred_element_type=F32)
        return carry

    lax.fori_loop(0, n_chunks, kv_body, 0)

    state = jnp.zeros((c, LANES), F32)
    for i in range(n_chunks):
        s_scr[i, 0:c] = state
        state = state * c_dec_f + u_scr[i, 0:c]
    order = list(range(n_ctx_chunks - 1, -1, -1)) + list(range(n_chunks - 1, n_ctx_chunks - 1, -1))
    state = jnp.zeros((c, LANES), F32)
    for i in order:
        s_scr[i, c:2 * c] = state
        state = state * c_dec_b + u_scr[i, c:2 * c]

    def out_body(i, carry):
        qc = q_ref[0, 0, chunk_rows(i), :]
        kc = k_ref[0, 0, chunk_rows(i), :]
        vc = v_ref[0, 0, chunk_rows(i), :]
        qf = qc.astype(F32)
        qq = jnp.concatenate([qf * q_dec_f, qf * q_dec_b], axis=1).astype(BF16)
        cross = jnp.dot(qq, s_scr[i].astype(BF16), preferred_element_type=F32)
        sc = lax.dot_general(qc, kc, (((1,), (1,)), ((), ())), preferred_element_type=F32) * decay
        o = jnp.dot(sc.astype(BF16), vc, preferred_element_type=F32) + cross
        mu = jnp.mean(o, axis=-1, keepdims=True)
        dev = o - mu
        var = jnp.mean(dev * dev, axis=-1, keepdims=True)
        g = g_ref[0, 0, chunk_rows(i), :].astype(F32)
        o_ref[0, chunk_rows(i), :] = (dev * lax.rsqrt(var + EPS) * _silu(g)).astype(BF16)
        return carry

    lax.fori_loop(0, n_chunks, out_body, 0)


def _retention(proj, decay_logit_lanes, n_ctx):
    b, _, l, _ = proj.shape
    n_chunks = l // RET_CHUNK

    def col(base):
        return pl.BlockSpec((1, 1, l, LANES), lambda i, h: (i, base + h, 0, 0))

    return pl.pallas_call(
        functools.partial(_ret_kernel, n_ctx_chunks=n_ctx // RET_CHUNK),
        out_shape=jax.ShapeDtypeStruct((b, l, RET_HEADS * LANES), BF16),
        grid=(b, RET_HEADS),
        in_specs=[
            pl.BlockSpec((1, 2, LANES), lambda i, h: (h, 0, 0)),
            col(0), col(RET_HEADS), col(2 * RET_HEADS), col(3 * RET_HEADS),
        ],
        out_specs=pl.BlockSpec((1, l, LANES), lambda i, h: (i, 0, h)),
        scratch_shapes=[
            pltpu.VMEM((n_chunks, 2 * RET_CHUNK, LANES), F32),
            pltpu.VMEM((n_chunks, 2 * RET_CHUNK, LANES), F32),
        ],
        compiler_params=_cparams(2),
        name="retention",
    )(decay_logit_lanes, proj, proj, proj, proj)


def _pool_kernel(p_ref, g_ref, w_ref, sc_ref, o_ref, pad_scr, *, segments):
    for grp, win in enumerate(POOL_WINDOWS):
        below = win // 2
        above = win - below - 1
        base = 0
        for start, length in segments:
            zeros = jnp.zeros((POOL_PAD, LANES), F32)
            pad_scr[base:base + POOL_PAD] = zeros
            pad_scr[base + POOL_PAD:base + POOL_PAD + length] = p_ref[0, grp, start:start + length, :].astype(F32)
            pad_scr[base + POOL_PAD + length:base + 2 * POOL_PAD + length] = zeros
            base += length + 2 * POOL_PAD
        base = 0
        for start, length in segments:
            first = base + POOL_PAD
            total = pad_scr[first - below:first - below + length]
            for off in range(-below + 1, above + 1):
                total = total + pad_scr[first + off:first + off + length]
            pos = lax.broadcasted_iota(jnp.int32, (length, LANES), 0)
            lo = jnp.maximum(pos - below, 0)
            hi = jnp.minimum(pos + above, length - 1)
            mean = total / (hi - lo + 1).astype(F32)
            d = mean - pad_scr[first:first + length]
            y = jnp.dot(d.astype(BF16), w_ref[grp], preferred_element_type=F32) * sc_ref[grp]
            g = g_ref[0, grp, start:start + length, :].astype(F32)
            o_ref[0, start:start + length, grp * LANES:(grp + 1) * LANES] = (y * _silu(g)).astype(BF16)
            base += length + 2 * POOL_PAD


def _pooling(proj, pool_w_bf16, pool_scale, segments):
    b, _, l, _ = proj.shape
    g = POOL_GROUPS
    pad_rows = l + 2 * POOL_PAD * len(segments)
    return pl.pallas_call(
        functools.partial(_pool_kernel, segments=segments),
        out_shape=jax.ShapeDtypeStruct((b, l, g * LANES), BF16),
        grid=(b,),
        in_specs=[
            pl.BlockSpec((1, g, l, LANES), lambda i: (i, 4, 0, 0)),
            pl.BlockSpec((1, g, l, LANES), lambda i: (i, 5, 0, 0)),
            pl.BlockSpec((g, LANES, LANES), lambda i: (0, 0, 0)),
            pl.BlockSpec((g, 1, LANES), lambda i: (0, 0, 0)),
        ],
        out_specs=pl.BlockSpec((1, l, g * LANES), lambda i: (i, 0, 0)),
        scratch_shapes=[pltpu.VMEM((pad_rows, LANES), F32)],
        compiler_params=_cparams(1),
        name="pool_mix",
    )(proj, proj, pool_w_bf16, pool_scale.reshape(g, 1, LANES))


def _attn_kernel(lam_ref, sub_ref, q_ref, k_ref, v_ref, g_ref, o_ref, *, n_ctx, out_scale, skip_ctx):
    lam = lam_ref[...][:, 0:1]
    q = q_ref[0, 0] * (DIFF_DH ** -0.5)
    lane = lax.broadcasted_iota(jnp.int32, q.shape, 1)
    zero = jnp.zeros_like(q)
    q_maps = (jnp.where(lane < DIFF_DH, q, zero), jnp.where(lane >= DIFF_DH, q, zero))

    def attend(n_keys):
        k = k_ref[0, 0, 0:n_keys, :]
        v = v_ref[0, 0, 0:n_keys, :]
        probs = []
        for qm in q_maps:
            s = lax.dot_general(qm, k, (((1,), (1,)), ((), ())), preferred_element_type=F32)
            p = jnp.exp(s - jnp.max(s, axis=-1, keepdims=True))
            probs.append((p, jnp.sum(p, axis=-1, keepdims=True)))
        (p0, l0), (p1, l1) = probs
        a = (p0 * (1.0 / l0) - p1 * (lam / l1)).astype(BF16)
        o = jnp.dot(a, v, preferred_element_type=F32)
        o = o * lax.rsqrt(jnp.mean(o * o, axis=-1, keepdims=True) + EPS) * sub_ref[...] * out_scale
        g = g_ref[0, 0].astype(F32)
        o_ref[0] = (o * _silu(g)).astype(BF16)

    if skip_ctx:
        attend(k_ref.shape[2])
        return
    j = pl.program_id(2)

    @pl.when(j == 0)
    def _():
        attend(n_ctx)

    @pl.when(j > 0)
    def _():
        attend(k_ref.shape[2])


def _diff_attention(proj, lam, subln, *, n_ctx, out_scale, skip_ctx):
    b, _, l, _ = proj.shape
    h = DIFF_HEADS
    skip = 1 if skip_ctx else 0
    kern = functools.partial(_attn_kernel, n_ctx=n_ctx, out_scale=out_scale, skip_ctx=skip_ctx)
    return pl.pallas_call(
        kern,
        out_shape=jax.ShapeDtypeStruct((b, l - skip * TM, h * LANES), BF16),
        grid=(b, h, l // TM - skip),
        in_specs=[
            pl.BlockSpec((1, LANES), lambda i, hh, j: (0, 0)),
            pl.BlockSpec((1, LANES), lambda i, hh, j: (0, 0)),
            pl.BlockSpec((1, 1, TM, LANES), lambda i, hh, j: (i, hh, j + skip, 0)),
            pl.BlockSpec((1, 1, l, LANES), lambda i, hh, j: (i, h + hh, 0, 0)),
            pl.BlockSpec((1, 1, l, LANES), lambda i, hh, j: (i, 2 * h + hh, 0, 0)),
            pl.BlockSpec((1, 1, TM, LANES), lambda i, hh, j: (i, 3 * h + hh, j + skip, 0)),
        ],
        out_specs=pl.BlockSpec((1, TM, LANES), lambda i, hh, j: (i, j, hh)),
        compiler_params=_cparams(3),
        name="diff_attention",
    )(lam, subln.reshape(1, LANES), proj, proj, proj, proj)


def _rope_tables(n_ctx, seq, axis_dim):
    t = jnp.arange(seq)
    rows = (t // GRID_W).astype(F32)
    cols = (t % GRID_W).astype(F32)
    freqs = ROPE_BASE ** (-jnp.arange(axis_dim // 2, dtype=F32) * 2.0 / axis_dim)
    lane = jnp.arange(LANES)
    use_cols = (lane % (2 * axis_dim)) // axis_dim == 1
    within = lane % axis_dim
    first = within < axis_dim // 2
    f = freqs[within % (axis_dim // 2)]
    pos = jnp.where(use_cols[None, :], cols[:, None], rows[:, None])
    ang = pos * f[None, :]
    cos, sin = jnp.cos(ang), jnp.sin(ang)
    sin_up = jnp.where(first[None, :], -sin, 0.0)
    sin_dn = jnp.where(first[None, :], 0.0, sin)
    pad = lambda a, fill: jnp.concatenate([jnp.full((n_ctx, LANES), fill, F32), a.astype(F32)], axis=0)
    return pad(cos, 1.0), pad(sin_up, 0.0), pad(sin_dn, 0.0)


def kernel(x, c, ctx, c_ctx, ada_w, ada_b, norm_pre, norm_post, ev_w_in, ev_w_out, ret_decay_logit, pool_w,
           pool_scale, od_w_in, od_w_out, diff_lambda, diff_subln):
    b, seq, d = x.shape
    n_ctx = ctx.shape[1]
    assert n_ctx == TM and seq % TM == 0 and d == D_MODEL

    xx = jnp.concatenate([ctx, x], axis=1)

    mod_rows = ((b + 1 + 7) // 8) * 8
    c_aug = jnp.zeros((mod_rows, d), F32).at[:b].set(c).at[b].set(c_ctx)
    mod = _modulation(c_aug, ada_w, ada_b)

    lam_inits = tuple(0.8 - 0.6 * math.exp(-0.3 * l) for l in range(1, DEPTH, 2))
    lams = _diff_lambdas(diff_lambda, lam_inits)

    even_tabs = _rope_tables(n_ctx, seq, RET_DK // 2)
    odd_tabs = _rope_tables(n_ctx, seq, DIFF_DH // 2)

    for l in range(DEPTH):
        last = l == DEPTH - 1
        i = l // 2
        m = mod[l].reshape(mod_rows, 3, d)
        modsel = jnp.stack([jnp.broadcast_to(m[b], (b, 3, d)), m[:b]], axis=1)
        g_pre = norm_pre[l].reshape(1, d)
        g_post = norm_post[l].reshape(1, d)
        if l % 2 == 0:
            proj = _project(xx, modsel, g_pre, ev_w_in[i].astype(BF16), even_tabs,
                            n_rope=2 * RET_HEADS, n_q=RET_HEADS, rot=RET_DK // 4, k_scale=RET_DK ** -0.5)
            dl = jnp.broadcast_to(ret_decay_logit[i].T[:, :, None], (RET_HEADS, 2, LANES))
            z_ret = _retention(proj, dl, n_ctx)
            z_pool = _pooling(proj, pool_w[i].astype(BF16), pool_scale[i], ((0, n_ctx), (n_ctx, seq)))
            xx = _out_project(z_ret, z_pool, 0, ev_w_out[i].astype(BF16), xx, modsel, g_post, latent_only=last)
        else:
            proj = _project(xx, modsel, g_pre, od_w_in[i].astype(BF16), odd_tabs,
                            n_rope=2 * DIFF_HEADS, n_q=2 * DIFF_HEADS, rot=DIFF_DH // 4, k_scale=1.0)
            z = _diff_attention(proj, lams[i], diff_subln[i], n_ctx=n_ctx, out_scale=1.0 - lam_inits[i],
                                skip_ctx=last)
            xx = _out_project(z, z, 1, od_w_out[i].astype(BF16), xx, modsel, g_post, latent_only=last)
    return xx
```

```python
import functools
import math

import jax
import jax.numpy as jnp
from jax import lax
from jax.experimental import pallas as pl
from jax.experimental.pallas import tpu as pltpu

F32 = jnp.float32
BF16 = jnp.bfloat16

D_MODEL = 1024
DEPTH = 4
GRID_W = 64
EPS = 1e-6
ROPE_BASE = 10000.0
LANES = 128

RET_HEADS = 4
RET_DK = 128
RET_CHUNK = 128
POOL_GROUPS = 4
POOL_WINDOWS = (2, 4, 8, 16)
POOL_PAD = 8
EVEN_IN = 3072
EVEN_NB = EVEN_IN // LANES

DIFF_HEADS = 8
DIFF_DH = 64
ATTN_HEADS_PER_STEP = 4
LOG2_E = 1.4426950408889634
ODD_IN = 4096
ODD_NB = ODD_IN // LANES

TM = 256
PROJ_TN = 512
VMEM_LIMIT = 48 * 1024 * 1024


def _cparams(n_axes):
    return pltpu.CompilerParams(dimension_semantics=("parallel",) * n_axes, vmem_limit_bytes=VMEM_LIMIT)


def _silu(x):
    return x * jax.nn.sigmoid(x)


def _mod_kernel(c_ref, w_ref, b_ref, o_ref):
    a = _silu(c_ref[...])
    o_ref[0] = jnp.dot(a, w_ref[0], preferred_element_type=F32, precision=lax.Precision.HIGHEST) + b_ref[0]


def _modulation(c_aug, ada_w, ada_b):
    rows = c_aug.shape[0]
    nt = 3 * D_MODEL // D_MODEL
    return pl.pallas_call(
        _mod_kernel,
        out_shape=jax.ShapeDtypeStruct((DEPTH, rows, 3 * D_MODEL), F32),
        grid=(DEPTH, nt),
        in_specs=[
            pl.BlockSpec((rows, D_MODEL), lambda l, n: (0, 0)),
            pl.BlockSpec((1, D_MODEL, D_MODEL), lambda l, n: (l, 0, n)),
            pl.BlockSpec((1, 1, D_MODEL), lambda l, n: (l, 0, n)),
        ],
        out_specs=pl.BlockSpec((1, rows, D_MODEL), lambda l, n: (l, 0, n)),
        compiler_params=_cparams(2),
        name="adaln_modulation",
    )(c_aug, ada_w, ada_b.reshape(DEPTH, 1, 3 * D_MODEL))


def _lam_kernel(lp_ref, o_ref, *, lam_inits):
    for i, lam_init in enumerate(lam_inits):
        lp = lp_ref[i]
        a = jnp.sum(lp[0:1] * lp[1:2], axis=-1, keepdims=True)
        b = jnp.sum(lp[2:3] * lp[3:4], axis=-1, keepdims=True)
        lam = jnp.exp(a) - jnp.exp(b) + lam_init
        o_ref[i] = jnp.broadcast_to(lam, (1, LANES))


def _diff_lambdas(diff_lambda, lam_inits):
    n = diff_lambda.shape[0]
    return pl.pallas_call(
        functools.partial(_lam_kernel, lam_inits=lam_inits),
        out_shape=jax.ShapeDtypeStruct((n, 1, LANES), F32),
        name="diff_lambda",
    )(diff_lambda)


def _proj_kernel(x_ref, mod_ref, g_ref, w_ref, c_ref, s1_ref, s2_ref, o_ref, *, n_rope, n_q, rot, k_scale):
    x = x_ref[0]
    m = mod_ref[0, 0]
    y = x * lax.rsqrt(jnp.mean(x * x, axis=-1, keepdims=True) + EPS) * g_ref[...]
    h = (y * (1.0 + m[1:2]) + m[0:1]).astype(BF16)
    cos, sin_up, sin_dn = c_ref[...], s1_ref[...], s2_ref[...]
    per = PROJ_TN // LANES
    for n in range(w_ref.shape[1] // PROJ_TN):
        r = jnp.dot(h, w_ref[:, n * PROJ_TN:(n + 1) * PROJ_TN], preferred_element_type=F32)
        for cb in range(per):
            blk = n * per + cb
            v = r[:, cb * LANES:(cb + 1) * LANES]
            if blk < n_rope:
                v = v * cos + pltpu.roll(v, LANES - rot, 1) * sin_up + pltpu.roll(v, rot, 1) * sin_dn
                if blk >= n_q:
                    v = v * k_scale
            o_ref[0, blk] = v.astype(BF16)


def _project(xx, modsel, g_pre, w_bf16, tables, *, n_rope, n_q, rot, k_scale):
    b, l, d = xx.shape
    n_in = w_bf16.shape[1]
    nb = n_in // LANES
    kern = functools.partial(_proj_kernel, n_rope=n_rope, n_q=n_q, rot=rot, k_scale=k_scale)
    tab_spec = pl.BlockSpec((TM, LANES), lambda i, j: (j, 0))
    return pl.pallas_call(
        kern,
        out_shape=jax.ShapeDtypeStruct((b, nb, l, LANES), BF16),
        grid=(b, l // TM),
        in_specs=[
            pl.BlockSpec((1, TM, d), lambda i, j: (i, j, 0)),
            pl.BlockSpec((1, 1, 3, d), lambda i, j: (i, jnp.minimum(j, 1), 0, 0)),
            pl.BlockSpec((1, d), lambda i, j: (0, 0)),
            pl.BlockSpec((d, n_in), lambda i, j: (0, 0)),
            tab_spec, tab_spec, tab_spec,
        ],
        out_specs=pl.BlockSpec((1, nb, TM, LANES), lambda i, j: (i, 0, j, 0)),
        compiler_params=_cparams(2),
        name="input_projection",
    )(xx, modsel, g_pre, w_bf16, *tables)


def _out_kernel(za_ref, zb_ref, w_ref, x_ref, mod_ref, g_ref, o_ref):
    half = za_ref.shape[-1]
    y = jnp.dot(za_ref[0], w_ref[0:half], preferred_element_type=F32)
    y = y + jnp.dot(zb_ref[0], w_ref[half:2 * half], preferred_element_type=F32)
    r = y * lax.rsqrt(jnp.mean(y * y, axis=-1, keepdims=True) + EPS) * g_ref[...]
    o_ref[0] = x_ref[0] + mod_ref[0, 0][2:3] * r


def _out_project(za, zb, zb_col, w_bf16, xx, modsel, g_post, *, latent_only):
    b, l, d = xx.shape
    half = d // 2
    skip = 1 if latent_only else 0
    z_skip = skip - (l - za.shape[1]) // TM
    n_blocks = l // TM - skip
    out_rows = n_blocks * TM
    return pl.pallas_call(
        _out_kernel,
        out_shape=jax.ShapeDtypeStruct((b, out_rows, d), F32),
        grid=(b, n_blocks),
        in_specs=[
            pl.BlockSpec((1, TM, half), lambda i, j: (i, j + z_skip, 0)),
            pl.BlockSpec((1, TM, half), lambda i, j: (i, j + z_skip, zb_col)),
            pl.BlockSpec((d, d), lambda i, j: (0, 0)),
            pl.BlockSpec((1, TM, d), lambda i, j: (i, j + skip, 0)),
            pl.BlockSpec((1, 1, 3, d), lambda i, j: (i, jnp.minimum(j + skip, 1), 0, 0)),
            pl.BlockSpec((1, d), lambda i, j: (0, 0)),
        ],
        out_specs=pl.BlockSpec((1, TM, d), lambda i, j: (i, j, 0)),
        input_output_aliases={} if latent_only else {3: 0},
        compiler_params=_cparams(2),
        name="output_projection",
    )(za, zb, w_bf16, xx, modsel, g_post)


def _ret_kernel(dl_ref, q_ref, k_ref, v_ref, g_ref, o_ref, *, n_ctx_chunks):
    c = RET_CHUNK
    n_chunks = q_ref.shape[2] // c
    lg = -jnp.exp(dl_ref[0])
    lgf, lgb = lg[0:1], lg[1:2]
    ri = lax.broadcasted_iota(jnp.int32, (c, c), 0).astype(F32)
    ci = lax.broadcasted_iota(jnp.int32, (c, c), 1).astype(F32)
    rel = ri - ci
    decay = jnp.where(rel >= 0.0, jnp.exp(lgf * jnp.maximum(rel, 0.0)), jnp.exp(lgb * jnp.maximum(-rel, 0.0)))
    q_dec_f = jnp.exp(lgf * (ri + 1.0))
    q_dec_b = jnp.exp(lgb * (c - ri))
    k_dec_f = jnp.exp(lgf * (c - 1.0 - ri))
    k_dec_b = jnp.exp(lgb * ri)
    c_dec_f = jnp.exp(lgf * c)
    c_dec_b = jnp.exp(lgb * c)

    def chunk(ref, i):
        return ref[0, 0, i * c:(i + 1) * c, :]

    updates = []
    for i in range(n_chunks):
        kc = chunk(k_ref, i).astype(F32)
        kk = jnp.concatenate([kc * k_dec_f, kc * k_dec_b], axis=1).astype(BF16)
        updates.append(lax.dot_general(kk, chunk(v_ref, i), (((0,), (0,)), ((), ())), preferred_element_type=F32))

    before = [None] * n_chunks
    state = jnp.zeros((c, LANES), F32)
    for i in range(n_chunks):
        before[i] = [state]
        state = state * c_dec_f + updates[i][0:c]
    order = list(range(n_ctx_chunks - 1, -1, -1)) + list(range(n_chunks - 1, n_ctx_chunks - 1, -1))
    state = jnp.zeros((c, LANES), F32)
    for i in order:
        before[i].append(state)
        state = state * c_dec_b + updates[i][c:2 * c]

    outs = []
    for i in range(n_chunks):
        qc, kc, vc = chunk(q_ref, i), chunk(k_ref, i), chunk(v_ref, i)
        qf = qc.astype(F32)
        qq = jnp.concatenate([qf * q_dec_f, qf * q_dec_b], axis=1).astype(BF16)
        states = jnp.concatenate(before[i], axis=0).astype(BF16)
        cross = jnp.dot(qq, states, preferred_element_type=F32)
        sc = lax.dot_general(qc, kc, (((1,), (1,)), ((), ())), preferred_element_type=F32) * decay
        o = jnp.dot(sc.astype(BF16), vc, preferred_element_type=F32) + cross
        mu = jnp.mean(o, axis=-1, keepdims=True)
        dev = o - mu
        var = jnp.mean(dev * dev, axis=-1, keepdims=True)
        g = chunk(g_ref, i).astype(F32)
        outs.append((dev * lax.rsqrt(var + EPS) * _silu(g)).astype(BF16))
    o_ref[0] = jnp.concatenate(outs, axis=0)


def _retention(proj, decay_logit_lanes, n_ctx):
    b, _, l, _ = proj.shape

    def col(base):
        return pl.BlockSpec((1, 1, l, LANES), lambda i, h: (i, base + h, 0, 0))

    return pl.pallas_call(
        functools.partial(_ret_kernel, n_ctx_chunks=n_ctx // RET_CHUNK),
        out_shape=jax.ShapeDtypeStruct((b, l, RET_HEADS * LANES), BF16),
        grid=(b, RET_HEADS),
        in_specs=[
            pl.BlockSpec((1, 2, LANES), lambda i, h: (h, 0, 0)),
            col(0), col(RET_HEADS), col(2 * RET_HEADS), col(3 * RET_HEADS),
        ],
        out_specs=pl.BlockSpec((1, l, LANES), lambda i, h: (i, 0, h)),
        compiler_params=_cparams(2),
        name="retention",
    )(decay_logit_lanes, proj, proj, proj, proj)


def _pool_kernel(p_ref, g_ref, w_ref, sc_ref, o_ref, pad_scr, *, segments):
    for grp, win in enumerate(POOL_WINDOWS):
        below = win // 2
        above = win - below - 1
        base = 0
        for start, length in segments:
            zeros = jnp.zeros((POOL_PAD, LANES), F32)
            pad_scr[base:base + POOL_PAD] = zeros
            pad_scr[base + POOL_PAD:base + POOL_PAD + length] = p_ref[0, grp, start:start + length, :].astype(F32)
            pad_scr[base + POOL_PAD + length:base + 2 * POOL_PAD + length] = zeros
            base += length + 2 * POOL_PAD
        base = 0
        for start, length in segments:
            first = base + POOL_PAD
            total = pad_scr[first - below:first - below + length]
            for off in range(-below + 1, above + 1):
                total = total + pad_scr[first + off:first + off + length]
            pos = lax.broadcasted_iota(jnp.int32, (length, LANES), 0)
            lo = jnp.maximum(pos - below, 0)
            hi = jnp.minimum(pos + above, length - 1)
            mean = total / (hi - lo + 1).astype(F32)
            d = mean - pad_scr[first:first + length]
            y = jnp.dot(d.astype(BF16), w_ref[grp], preferred_element_type=F32) * sc_ref[grp]
            g = g_ref[0, grp, start:start + length, :].astype(F32)
            o_ref[0, start:start + length, grp * LANES:(grp + 1) * LANES] = (y * _silu(g)).astype(BF16)
            base += length + 2 * POOL_PAD


def _pooling(proj, pool_w_bf16, pool_scale, segments):
    b, _, l, _ = proj.shape
    g = POOL_GROUPS
    pad_rows = l + 2 * POOL_PAD * len(segments)
    return pl.pallas_call(
        functools.partial(_pool_kernel, segments=segments),
        out_shape=jax.ShapeDtypeStruct((b, l, g * LANES), BF16),
        grid=(b,),
        in_specs=[
            pl.BlockSpec((1, g, l, LANES), lambda i: (i, 4, 0, 0)),
            pl.BlockSpec((1, g, l, LANES), lambda i: (i, 5, 0, 0)),
            pl.BlockSpec((g, LANES, LANES), lambda i: (0, 0, 0)),
            pl.BlockSpec((g, 1, LANES), lambda i: (0, 0, 0)),
        ],
        out_specs=pl.BlockSpec((1, l, g * LANES), lambda i: (i, 0, 0)),
        scratch_shapes=[pltpu.VMEM((pad_rows, LANES), F32)],
        compiler_params=_cparams(1),
        name="pool_mix",
    )(proj, proj, pool_w_bf16, pool_scale.reshape(g, 1, LANES))


def _attn_kernel(lam_ref, sub_ref, q_ref, k_ref, v_ref, g_ref, o_ref, *, n_ctx, out_scale, skip_ctx):
    lam = lam_ref[...][:, 0:1]
    lane = lax.broadcasted_iota(jnp.int32, (q_ref.shape[2], LANES), 1)

    def attend(n_keys):
        def score_maps(hh):
            q = q_ref[0, hh] * (DIFF_DH ** -0.5 * LOG2_E)
            zero = jnp.zeros_like(q)
            k = k_ref[0, hh, 0:n_keys, :]
            return [lax.dot_general(qm, k, (((1,), (1,)), ((), ())), preferred_element_type=F32)
                    for qm in (jnp.where(lane < DIFF_DH, q, zero), jnp.where(lane >= DIFF_DH, q, zero))]

        n_heads = q_ref.shape[1]
        outs = []
        nxt = score_maps(0)
        for hh in range(n_heads):
            cur = nxt
            if hh + 1 < n_heads:
                nxt = score_maps(hh + 1)
            v = v_ref[0, hh, 0:n_keys, :]
            probs = []
            for s in cur:
                p = jnp.exp2(s - jnp.max(s, axis=-1, keepdims=True))
                probs.append((p, jnp.sum(p, axis=-1, keepdims=True)))
            (p0, l0), (p1, l1) = probs
            a = (p0 - p1 * (lam * l0 / l1)).astype(BF16)
            o = jnp.dot(a, v, preferred_element_type=F32) * (1.0 / l0)
            o = o * lax.rsqrt(jnp.mean(o * o, axis=-1, keepdims=True) + EPS) * sub_ref[...] * out_scale
            g = g_ref[0, hh].astype(F32)
            outs.append((o * _silu(g)).astype(BF16))
        o_ref[0] = jnp.concatenate(outs, axis=1)

    if skip_ctx:
        attend(k_ref.shape[2])
        return
    j = pl.program_id(2)

    @pl.when(j == 0)
    def _():
        attend(n_ctx)

    @pl.when(j > 0)
    def _():
        attend(k_ref.shape[2])


def _diff_attention(proj, lam, subln, *, n_ctx, out_scale, skip_ctx):
    b, _, l, _ = proj.shape
    hp = ATTN_HEADS_PER_STEP
    ng = DIFF_HEADS // hp
    skip = 1 if skip_ctx else 0
    kern = functools.partial(_attn_kernel, n_ctx=n_ctx, out_scale=out_scale, skip_ctx=skip_ctx)
    return pl.pallas_call(
        kern,
        out_shape=jax.ShapeDtypeStruct((b, l - skip * TM, DIFF_HEADS * LANES), BF16),
        grid=(b, ng, l // TM - skip),
        in_specs=[
            pl.BlockSpec((1, LANES), lambda i, hh, j: (0, 0)),
            pl.BlockSpec((1, LANES), lambda i, hh, j: (0, 0)),
            pl.BlockSpec((1, hp, TM, LANES), lambda i, hh, j: (i, hh, j + skip, 0)),
            pl.BlockSpec((1, hp, l, LANES), lambda i, hh, j: (i, ng + hh, 0, 0)),
            pl.BlockSpec((1, hp, l, LANES), lambda i, hh, j: (i, 2 * ng + hh, 0, 0)),
            pl.BlockSpec((1, hp, TM, LANES), lambda i, hh, j: (i, 3 * ng + hh, j + skip, 0)),
        ],
        out_specs=pl.BlockSpec((1, TM, hp * LANES), lambda i, hh, j: (i, j, hh)),
        compiler_params=_cparams(3),
        name="diff_attention",
    )(lam, subln.reshape(1, LANES), proj, proj, proj, proj)


def _rope_tables(n_ctx, seq, axis_dim):
    t = jnp.arange(seq)
    rows = (t // GRID_W).astype(F32)
    cols = (t % GRID_W).astype(F32)
    freqs = ROPE_BASE ** (-jnp.arange(axis_dim // 2, dtype=F32) * 2.0 / axis_dim)
    lane = jnp.arange(LANES)
    use_cols = (lane % (2 * axis_dim)) // axis_dim == 1
    within = lane % axis_dim
    first = within < axis_dim // 2
    f = freqs[within % (axis_dim // 2)]
    pos = jnp.where(use_cols[None, :], cols[:, None], rows[:, None])
    ang = pos * f[None, :]
    cos, sin = jnp.cos(ang), jnp.sin(ang)
    sin_up = jnp.where(first[None, :], -sin, 0.0)
    sin_dn = jnp.where(first[None, :], 0.0, sin)
    pad = lambda a, fill: jnp.concatenate([jnp.full((n_ctx, LANES), fill, F32), a.astype(F32)], axis=0)
    return pad(cos, 1.0), pad(sin_up, 0.0), pad(sin_dn, 0.0)


def kernel(x, c, ctx, c_ctx, ada_w, ada_b, norm_pre, norm_post, ev_w_in, ev_w_out, ret_decay_logit, pool_w,
           pool_scale, od_w_in, od_w_out, diff_lambda, diff_subln):
    b, seq, d = x.shape
    n_ctx = ctx.shape[1]
    assert n_ctx == TM and seq % TM == 0 and d == D_MODEL

    xx = jnp.concatenate([ctx, x], axis=1)

    mod_rows = ((b + 1 + 7) // 8) * 8
    c_aug = jnp.zeros((mod_rows, d), F32).at[:b].set(c).at[b].set(c_ctx)
    mod = _modulation(c_aug, ada_w, ada_b)

    lam_inits = tuple(0.8 - 0.6 * math.exp(-0.3 * l) for l in range(1, DEPTH, 2))
    lams = _diff_lambdas(diff_lambda, lam_inits)

    even_tabs = _rope_tables(n_ctx, seq, RET_DK // 2)
    odd_tabs = _rope_tables(n_ctx, seq, DIFF_DH // 2)

    for l in range(DEPTH):
        last = l == DEPTH - 1
        i = l // 2
        m = mod[l].reshape(mod_rows, 3, d)
        modsel = jnp.stack([jnp.broadcast_to(m[b], (b, 3, d)), m[:b]], axis=1)
        g_pre = norm_pre[l].reshape(1, d)
        g_post = norm_post[l].reshape(1, d)
        if l % 2 == 0:
            proj = _project(xx, modsel, g_pre, ev_w_in[i].astype(BF16), even_tabs,
                            n_rope=2 * RET_HEADS, n_q=RET_HEADS, rot=RET_DK // 4, k_scale=RET_DK ** -0.5)
            dl = jnp.broadcast_to(ret_decay_logit[i].T[:, :, None], (RET_HEADS, 2, LANES))
            z_ret = _retention(proj, dl, n_ctx)
            z_pool = _pooling(proj, pool_w[i].astype(BF16), pool_scale[i], ((0, n_ctx), (n_ctx, seq)))
            xx = _out_project(z_ret, z_pool, 0, ev_w_out[i].astype(BF16), xx, modsel, g_post, latent_only=last)
        else:
            proj = _project(xx, modsel, g_pre, od_w_in[i].astype(BF16), odd_tabs,
                            n_rope=2 * DIFF_HEADS, n_q=2 * DIFF_HEADS, rot=DIFF_DH // 4, k_scale=1.0)
            z = _diff_attention(proj, lams[i], diff_subln[i], n_ctx=n_ctx, out_scale=1.0 - lam_inits[i],
                                skip_ctx=last)
            xx = _out_project(z, z, 1, od_w_out[i].astype(BF16), xx, modsel, g_post, latent_only=last)
    return xx
```

```python
import functools
import math

import jax
import jax.numpy as jnp
from jax import lax
from jax.experimental import pallas as pl
from jax.experimental.pallas import tpu as pltpu

F32 = jnp.float32
BF16 = jnp.bfloat16

D_MODEL = 1024
DEPTH = 4
GRID_W = 64
EPS = 1e-6
ROPE_BASE = 10000.0
LANES = 128

RET_HEADS = 4
RET_DK = 128
RET_CHUNK = 128
RET_HEADS_PER_STEP = 2
POOL_GROUPS = 4
POOL_WINDOWS = (2, 4, 8, 16)
POOL_PAD = 8
EVEN_IN = 3072
EVEN_NB = EVEN_IN // LANES

DIFF_HEADS = 8
DIFF_DH = 64
ATTN_HEADS_PER_STEP = 4
ATTN_KEY_CHUNK = 768
LOG2_E = 1.4426950408889634
ODD_IN = 4096
ODD_NB = ODD_IN // LANES

TM = 256
OUT_TM = 768
PROJ_TN = 512
VMEM_LIMIT = 48 * 1024 * 1024


def _cparams(n_axes):
    return pltpu.CompilerParams(dimension_semantics=("parallel",) * n_axes, vmem_limit_bytes=VMEM_LIMIT)


def _silu(x):
    return x * jax.nn.sigmoid(x)


def _mod_kernel(c_ref, w_ref, b_ref, o_ref):
    a = _silu(c_ref[...])
    o_ref[0] = jnp.dot(a, w_ref[0], preferred_element_type=F32, precision=lax.Precision.HIGHEST) + b_ref[0]


def _modulation(c_aug, ada_w, ada_b):
    rows = c_aug.shape[0]
    nt = 3 * D_MODEL // D_MODEL
    return pl.pallas_call(
        _mod_kernel,
        out_shape=jax.ShapeDtypeStruct((DEPTH, rows, 3 * D_MODEL), F32),
        grid=(DEPTH, nt),
        in_specs=[
            pl.BlockSpec((rows, D_MODEL), lambda l, n: (0, 0)),
            pl.BlockSpec((1, D_MODEL, D_MODEL), lambda l, n: (l, 0, n)),
            pl.BlockSpec((1, 1, D_MODEL), lambda l, n: (l, 0, n)),
        ],
        out_specs=pl.BlockSpec((1, rows, D_MODEL), lambda l, n: (l, 0, n)),
        compiler_params=_cparams(2),
        name="adaln_modulation",
    )(c_aug, ada_w, ada_b.reshape(DEPTH, 1, 3 * D_MODEL))


def _lam_kernel(lp_ref, o_ref, *, lam_inits):
    for i, lam_init in enumerate(lam_inits):
        lp = lp_ref[i]
        a = jnp.sum(lp[0:1] * lp[1:2], axis=-1, keepdims=True)
        b = jnp.sum(lp[2:3] * lp[3:4], axis=-1, keepdims=True)
        lam = jnp.exp(a) - jnp.exp(b) + lam_init
        o_ref[i] = jnp.broadcast_to(lam, (1, LANES))


def _diff_lambdas(diff_lambda, lam_inits):
    n = diff_lambda.shape[0]
    return pl.pallas_call(
        functools.partial(_lam_kernel, lam_inits=lam_inits),
        out_shape=jax.ShapeDtypeStruct((n, 1, LANES), F32),
        name="diff_lambda",
    )(diff_lambda)


def _proj_kernel(x_ref, mod_ref, g_ref, w_ref, c_ref, s1_ref, s2_ref, o_ref, *, n_rope, n_q, rot, k_scale):
    x = x_ref[0]
    m = mod_ref[0, 0]
    y = x * lax.rsqrt(jnp.mean(x * x, axis=-1, keepdims=True) + EPS) * g_ref[...]
    h = (y * (1.0 + m[1:2]) + m[0:1]).astype(BF16)
    cos, sin_up, sin_dn = c_ref[...], s1_ref[...], s2_ref[...]
    per = PROJ_TN // LANES
    for n in range(w_ref.shape[1] // PROJ_TN):
        r = jnp.dot(h, w_ref[:, n * PROJ_TN:(n + 1) * PROJ_TN], preferred_element_type=F32)
        for cb in range(per):
            blk = n * per + cb
            v = r[:, cb * LANES:(cb + 1) * LANES]
            if blk < n_rope:
                v = v * cos + pltpu.roll(v, LANES - rot, 1) * sin_up + pltpu.roll(v, rot, 1) * sin_dn
                if blk >= n_q:
                    v = v * k_scale
            o_ref[0, blk] = v.astype(BF16)


def _project(xx, modsel, g_pre, w_bf16, tables, *, n_rope, n_q, rot, k_scale):
    b, l, d = xx.shape
    n_in = w_bf16.shape[1]
    nb = n_in // LANES
    kern = functools.partial(_proj_kernel, n_rope=n_rope, n_q=n_q, rot=rot, k_scale=k_scale)
    tab_spec = pl.BlockSpec((TM, LANES), lambda i, j: (j, 0))
    return pl.pallas_call(
        kern,
        out_shape=jax.ShapeDtypeStruct((b, nb, l, LANES), BF16),
        grid=(b, l // TM),
        in_specs=[
            pl.BlockSpec((1, TM, d), lambda i, j: (i, j, 0)),
            pl.BlockSpec((1, 1, 3, d), lambda i, j: (i, jnp.minimum(j, 1), 0, 0)),
            pl.BlockSpec((1, d), lambda i, j: (0, 0)),
            pl.BlockSpec((d, n_in), lambda i, j: (0, 0)),
            tab_spec, tab_spec, tab_spec,
        ],
        out_specs=pl.BlockSpec((1, nb, TM, LANES), lambda i, j: (i, 0, j, 0)),
        compiler_params=_cparams(2),
        name="input_projection",
    )(xx, modsel, g_pre, w_bf16, *tables)


def _out_kernel(za_ref, zb_ref, w_ref, x_ref, mod_ref, g_ref, o_ref, *, first_row, n_ctx):
    half = za_ref.shape[-1]
    tm = x_ref.shape[1]
    y = jnp.dot(za_ref[0], w_ref[0:half], preferred_element_type=F32)
    y = y + jnp.dot(zb_ref[0], w_ref[half:2 * half], preferred_element_type=F32)
    r = y * lax.rsqrt(jnp.mean(y * y, axis=-1, keepdims=True) + EPS) * g_ref[...]
    row = first_row + pl.program_id(1) * tm + lax.broadcasted_iota(jnp.int32, (tm, 1), 0)
    gate = jnp.where(row < n_ctx, mod_ref[0, 0][2:3], mod_ref[0, 1][2:3])
    o_ref[0] = x_ref[0] + gate * r


def _out_project(za, zb, zb_col, w_bf16, xx, modsel, g_post, *, n_ctx, latent_only):
    b, l, d = xx.shape
    half = d // 2
    tm = n_ctx if latent_only else OUT_TM
    skip = 1 if latent_only else 0
    z_skip = skip - (l - za.shape[1]) // tm
    n_blocks = l // tm - skip
    out_rows = n_blocks * tm
    return pl.pallas_call(
        functools.partial(_out_kernel, first_row=skip * tm, n_ctx=n_ctx),
        out_shape=jax.ShapeDtypeStruct((b, out_rows, d), F32),
        grid=(b, n_blocks),
        in_specs=[
            pl.BlockSpec((1, tm, half), lambda i, j: (i, j + z_skip, 0)),
            pl.BlockSpec((1, tm, half), lambda i, j: (i, j + z_skip, zb_col)),
            pl.BlockSpec((d, d), lambda i, j: (0, 0)),
            pl.BlockSpec((1, tm, d), lambda i, j: (i, j + skip, 0)),
            pl.BlockSpec((1, 2, 3, d), lambda i, j: (i, 0, 0, 0)),
            pl.BlockSpec((1, d), lambda i, j: (0, 0)),
        ],
        out_specs=pl.BlockSpec((1, tm, d), lambda i, j: (i, j, 0)),
        input_output_aliases={} if latent_only else {3: 0},
        compiler_params=_cparams(2),
        name="output_projection",
    )(za, zb, w_bf16, xx, modsel, g_post)


def _ret_kernel(dl_ref, q_ref, k_ref, v_ref, g_ref, o_ref, *, n_ctx_chunks):
    o_ref[0] = jnp.concatenate(
        [_ret_head(dl_ref, q_ref, k_ref, v_ref, g_ref, hh, n_ctx_chunks) for hh in range(q_ref.shape[1])], axis=1)


def _ret_head(dl_ref, q_ref, k_ref, v_ref, g_ref, hh, n_ctx_chunks):
    c = RET_CHUNK
    n_chunks = q_ref.shape[2] // c
    lg = -jnp.exp(dl_ref[hh])
    lgf, lgb = lg[0:1], lg[1:2]
    ri = lax.broadcasted_iota(jnp.int32, (c, c), 0).astype(F32)
    ci = lax.broadcasted_iota(jnp.int32, (c, c), 1).astype(F32)
    rel = ri - ci
    decay = jnp.where(rel >= 0.0, jnp.exp(lgf * jnp.maximum(rel, 0.0)), jnp.exp(lgb * jnp.maximum(-rel, 0.0)))
    q_dec_f = jnp.exp(lgf * (ri + 1.0))
    q_dec_b = jnp.exp(lgb * (c - ri))
    k_dec_f = jnp.exp(lgf * (c - 1.0 - ri))
    k_dec_b = jnp.exp(lgb * ri)
    c_dec_f = jnp.exp(lgf * c)
    c_dec_b = jnp.exp(lgb * c)

    def chunk(ref, i):
        return ref[0, hh, i * c:(i + 1) * c, :]

    updates = []
    for i in range(n_chunks):
        kc = chunk(k_ref, i).astype(F32)
        kk = jnp.concatenate([kc * k_dec_f, kc * k_dec_b], axis=1).astype(BF16)
        updates.append(lax.dot_general(kk, chunk(v_ref, i), (((0,), (0,)), ((), ())), preferred_element_type=F32))

    before = [None] * n_chunks
    state = jnp.zeros((c, LANES), F32)
    for i in range(n_chunks):
        before[i] = [state]
        state = state * c_dec_f + updates[i][0:c]
    order = list(range(n_ctx_chunks - 1, -1, -1)) + list(range(n_chunks - 1, n_ctx_chunks - 1, -1))
    state = jnp.zeros((c, LANES), F32)
    for i in order:
        before[i].append(state)
        state = state * c_dec_b + updates[i][c:2 * c]

    masked = [(lax.dot_general(chunk(q_ref, i), chunk(k_ref, i), (((1,), (1,)), ((), ())),
                               preferred_element_type=F32) * decay).astype(BF16) for i in range(n_chunks)]
    crosses = []
    for i in range(n_chunks):
        qf = chunk(q_ref, i).astype(F32)
        qq = jnp.concatenate([qf * q_dec_f, qf * q_dec_b], axis=1).astype(BF16)
        states = jnp.concatenate(before[i], axis=0).astype(BF16)
        crosses.append(jnp.dot(qq, states, preferred_element_type=F32))
    o = jnp.concatenate([jnp.dot(masked[i], chunk(v_ref, i), preferred_element_type=F32) + crosses[i]
                         for i in range(n_chunks)], axis=0)
    mu = jnp.mean(o, axis=-1, keepdims=True)
    dev = o - mu
    var = jnp.mean(dev * dev, axis=-1, keepdims=True)
    g = g_ref[0, hh].astype(F32)
    return (dev * lax.rsqrt(var + EPS) * _silu(g)).astype(BF16)


def _retention(proj, decay_logit_lanes, n_ctx):
    b, _, l, _ = proj.shape
    hp = RET_HEADS_PER_STEP
    ng = RET_HEADS // hp

    def col(kind):
        return pl.BlockSpec((1, hp, l, LANES), lambda i, h: (i, kind * ng + h, 0, 0))

    return pl.pallas_call(
        functools.partial(_ret_kernel, n_ctx_chunks=n_ctx // RET_CHUNK),
        out_shape=jax.ShapeDtypeStruct((b, l, RET_HEADS * LANES), BF16),
        grid=(b, ng),
        in_specs=[
            pl.BlockSpec((hp, 2, LANES), lambda i, h: (h, 0, 0)),
            col(0), col(1), col(2), col(3),
        ],
        out_specs=pl.BlockSpec((1, l, hp * LANES), lambda i, h: (i, 0, h)),
        compiler_params=_cparams(2),
        name="retention",
    )(decay_logit_lanes, proj, proj, proj, proj)


def _pool_kernel(p_ref, g_ref, w_ref, sc_ref, o_ref, pad_scr, *, segments):
    for grp, win in enumerate(POOL_WINDOWS):
        below = win // 2
        above = win - below - 1
        base = 0
        for start, length in segments:
            zeros = jnp.zeros((POOL_PAD, LANES), F32)
            pad_scr[base:base + POOL_PAD] = zeros
            pad_scr[base + POOL_PAD:base + POOL_PAD + length] = p_ref[0, grp, start:start + length, :].astype(F32)
            pad_scr[base + POOL_PAD + length:base + 2 * POOL_PAD + length] = zeros
            base += length + 2 * POOL_PAD
        base = 0
        for start, length in segments:
            first = base + POOL_PAD
            total = pad_scr[first - below:first - below + length]
            for off in range(-below + 1, above + 1):
                total = total + pad_scr[first + off:first + off + length]
            pos = lax.broadcasted_iota(jnp.int32, (length, LANES), 0)
            lo = jnp.maximum(pos - below, 0)
            hi = jnp.minimum(pos + above, length - 1)
            mean = total / (hi - lo + 1).astype(F32)
            d = mean - pad_scr[first:first + length]
            y = jnp.dot(d.astype(BF16), w_ref[grp], preferred_element_type=F32) * sc_ref[grp]
            g = g_ref[0, grp, start:start + length, :].astype(F32)
            o_ref[0, start:start + length, grp * LANES:(grp + 1) * LANES] = (y * _silu(g)).astype(BF16)
            base += length + 2 * POOL_PAD


def _pooling(proj, pool_w_bf16, pool_scale, segments):
    b, _, l, _ = proj.shape
    g = POOL_GROUPS
    pad_rows = l + 2 * POOL_PAD * len(segments)
    return pl.pallas_call(
        functools.partial(_pool_kernel, segments=segments),
        out_shape=jax.ShapeDtypeStruct((b, l, g * LANES), BF16),
        grid=(b,),
        in_specs=[
            pl.BlockSpec((1, g, l, LANES), lambda i: (i, 4, 0, 0)),
            pl.BlockSpec((1, g, l, LANES), lambda i: (i, 5, 0, 0)),
            pl.BlockSpec((g, LANES, LANES), lambda i: (0, 0, 0)),
            pl.BlockSpec((g, 1, LANES), lambda i: (0, 0, 0)),
        ],
        out_specs=pl.BlockSpec((1, l, g * LANES), lambda i: (i, 0, 0)),
        scratch_shapes=[pltpu.VMEM((pad_rows, LANES), F32)],
        compiler_params=_cparams(1),
        name="pool_mix",
    )(proj, proj, pool_w_bf16, pool_scale.reshape(g, 1, LANES))


def _attn_kernel(lam_ref, sub_ref, q_ref, k_ref, v_ref, g_ref, o_ref, *, n_ctx, out_scale, skip_ctx):
    lam = lam_ref[...][:, 0:1]
    lane = lax.broadcasted_iota(jnp.int32, (q_ref.shape[2], LANES), 1)

    def attend(n_keys):
        n_heads, tq = q_ref.shape[1], q_ref.shape[2]
        chunk = min(ATTN_KEY_CHUNK, n_keys)
        n_chunks = n_keys // chunk

        def stacked_q(hh):
            q = q_ref[0, hh] * (DIFF_DH ** -0.5 * LOG2_E)
            zero = jnp.zeros_like(q)
            return jnp.concatenate([jnp.where(lane < DIFF_DH, q, zero), jnp.where(lane >= DIFF_DH, q, zero)], axis=0)

        def keys(t):
            return slice(t * chunk, (t + 1) * chunk)

        outs = []
        cur_s, cur_m = None, None
        for hh in range(n_heads + 1):
            qs = stacked_q(hh) if hh < n_heads else None
            nxt_s, m_acc, probs, l_acc = [], None, [], None
            for t in range(n_chunks):
                if qs is not None:
                    s_t = lax.dot_general(k_ref[0, hh, keys(t), :], qs, (((1,), (1,)), ((), ())),
                                          preferred_element_type=F32)
                    nxt_s.append(s_t)
                    m_t = jnp.max(s_t, axis=0, keepdims=True)
                    m_acc = m_t if m_acc is None else jnp.maximum(m_acc, m_t)
                if cur_s is not None:
                    p_t = jnp.exp2(cur_s[t] - cur_m)
                    probs.append(p_t)
                    l_t = jnp.sum(p_t, axis=0, keepdims=True)
                    l_acc = l_t if l_acc is None else l_acc + l_t
            if cur_s is not None:
                l0, l1 = l_acc[:, 0:tq], l_acc[:, tq:2 * tq]
                ratio = lam * l0 / l1
                o_t = None
                for t in range(n_chunks):
                    a_t = (probs[t][:, 0:tq] - probs[t][:, tq:2 * tq] * ratio).astype(BF16)
                    part = lax.dot_general(v_ref[0, hh - 1, keys(t), :], a_t, (((0,), (0,)), ((), ())),
                                           preferred_element_type=F32)
                    o_t = part if o_t is None else o_t + part
                o_t = o_t * (1.0 / l0)
                o_t = o_t * lax.rsqrt(jnp.mean(o_t * o_t, axis=0, keepdims=True) + EPS)
                o = o_t.T * sub_ref[...] * out_scale
                g = g_ref[0, hh - 1].astype(F32)
                outs.append((o * _silu(g)).astype(BF16))
            if qs is not None:
                cur_s, cur_m = nxt_s, m_acc
        o_ref[0] = jnp.concatenate(outs, axis=1)

    if skip_ctx:
        attend(k_ref.shape[2])
        return
    j = pl.program_id(2)

    @pl.when(j == 0)
    def _():
        attend(n_ctx)

    @pl.when(j > 0)
    def _():
        attend(k_ref.shape[2])


def _diff_attention(proj, lam, subln, *, n_ctx, out_scale, skip_ctx):
    b, _, l, _ = proj.shape
    hp = ATTN_HEADS_PER_STEP
    ng = DIFF_HEADS // hp
    skip = 1 if skip_ctx else 0
    kern = functools.partial(_attn_kernel, n_ctx=n_ctx, out_scale=out_scale, skip_ctx=skip_ctx)
    return pl.pallas_call(
        kern,
        out_shape=jax.ShapeDtypeStruct((b, l - skip * TM, DIFF_HEADS * LANES), BF16),
        grid=(b, ng, l // TM - skip),
        in_specs=[
            pl.BlockSpec((1, LANES), lambda i, hh, j: (0, 0)),
            pl.BlockSpec((1, LANES), lambda i, hh, j: (0, 0)),
            pl.BlockSpec((1, hp, TM, LANES), lambda i, hh, j: (i, hh, j + skip, 0)),
            pl.BlockSpec((1, hp, l, LANES), lambda i, hh, j: (i, ng + hh, 0, 0)),
            pl.BlockSpec((1, hp, l, LANES), lambda i, hh, j: (i, 2 * ng + hh, 0, 0)),
            pl.BlockSpec((1, hp, TM, LANES), lambda i, hh, j: (i, 3 * ng + hh, j + skip, 0)),
        ],
        out_specs=pl.BlockSpec((1, TM, hp * LANES), lambda i, hh, j: (i, j, hh)),
        compiler_params=_cparams(3),
        name="diff_attention",
    )(lam, subln.reshape(1, LANES), proj, proj, proj, proj)


def _rope_tables(n_ctx, seq, axis_dim):
    t = jnp.arange(seq)
    rows = (t // GRID_W).astype(F32)
    cols = (t % GRID_W).astype(F32)
    freqs = ROPE_BASE ** (-jnp.arange(axis_dim // 2, dtype=F32) * 2.0 / axis_dim)
    lane = jnp.arange(LANES)
    use_cols = (lane % (2 * axis_dim)) // axis_dim == 1
    within = lane % axis_dim
    first = within < axis_dim // 2
    f = freqs[within % (axis_dim // 2)]
    pos = jnp.where(use_cols[None, :], cols[:, None], rows[:, None])
    ang = pos * f[None, :]
    cos, sin = jnp.cos(ang), jnp.sin(ang)
    sin_up = jnp.where(first[None, :], -sin, 0.0)
    sin_dn = jnp.where(first[None, :], 0.0, sin)
    pad = lambda a, fill: jnp.concatenate([jnp.full((n_ctx, LANES), fill, F32), a.astype(F32)], axis=0)
    return pad(cos, 1.0), pad(sin_up, 0.0), pad(sin_dn, 0.0)


def kernel(x, c, ctx, c_ctx, ada_w, ada_b, norm_pre, norm_post, ev_w_in, ev_w_out, ret_decay_logit, pool_w,
           pool_scale, od_w_in, od_w_out, diff_lambda, diff_subln):
    b, seq, d = x.shape
    n_ctx = ctx.shape[1]
    assert n_ctx == TM and seq % TM == 0 and (n_ctx + seq) % OUT_TM == 0 and d == D_MODEL

    xx = jnp.concatenate([ctx, x], axis=1)

    mod_rows = ((b + 1 + 7) // 8) * 8
    c_aug = jnp.zeros((mod_rows, d), F32).at[:b].set(c).at[b].set(c_ctx)
    mod = _modulation(c_aug, ada_w, ada_b)

    lam_inits = tuple(0.8 - 0.6 * math.exp(-0.3 * l) for l in range(1, DEPTH, 2))
    lams = _diff_lambdas(diff_lambda, lam_inits)

    even_tabs = _rope_tables(n_ctx, seq, RET_DK // 2)
    odd_tabs = _rope_tables(n_ctx, seq, DIFF_DH // 2)

    for l in range(DEPTH):
        last = l == DEPTH - 1
        i = l // 2
        m = mod[l].reshape(mod_rows, 3, d)
        modsel = jnp.stack([jnp.broadcast_to(m[b], (b, 3, d)), m[:b]], axis=1)
        g_pre = norm_pre[l].reshape(1, d)
        g_post = norm_post[l].reshape(1, d)
        if l % 2 == 0:
            proj = _project(xx, modsel, g_pre, ev_w_in[i].astype(BF16), even_tabs,
                            n_rope=2 * RET_HEADS, n_q=RET_HEADS, rot=RET_DK // 4, k_scale=RET_DK ** -0.5)
            dl = jnp.broadcast_to(ret_decay_logit[i].T[:, :, None], (RET_HEADS, 2, LANES))
            z_ret = _retention(proj, dl, n_ctx)
            z_pool = _pooling(proj, pool_w[i].astype(BF16), pool_scale[i], ((0, n_ctx), (n_ctx, seq)))
            xx = _out_project(z_ret, z_pool, 0, ev_w_out[i].astype(BF16), xx, modsel, g_post, n_ctx=n_ctx,
                              latent_only=last)
        else:
            proj = _project(xx, modsel, g_pre, od_w_in[i].astype(BF16), odd_tabs,
                            n_rope=2 * DIFF_HEADS, n_q=2 * DIFF_HEADS, rot=DIFF_DH // 4, k_scale=1.0)
            z = _diff_attention(proj, lams[i], diff_subln[i], n_ctx=n_ctx, out_scale=1.0 - lam_inits[i],
                                skip_ctx=last)
            xx = _out_project(z, z, 1, od_w_out[i].astype(BF16), xx, modsel, g_post, n_ctx=n_ctx, latent_only=last)
    return xx
```

```python
import functools
import math

import jax
import jax.numpy as jnp
from jax import lax
from jax.experimental import pallas as pl
from jax.experimental.pallas import tpu as pltpu

F32 = jnp.float32
BF16 = jnp.bfloat16

D_MODEL = 1024
DEPTH = 4
GRID_W = 64
EPS = 1e-6
ROPE_BASE = 10000.0
LANES = 128

RET_HEADS = 4
RET_DK = 128
RET_CHUNK = 128
RET_HEADS_PER_STEP = 2
POOL_GROUPS = 4
POOL_WINDOWS = (2, 4, 8, 16)
POOL_PAD = 8
EVEN_IN = 3072
EVEN_NB = EVEN_IN // LANES

DIFF_HEADS = 8
DIFF_DH = 64
ATTN_HEADS_PER_STEP = 8
ATTN_KEY_CHUNK = 768
LOG2_E = 1.4426950408889634
ODD_IN = 4096
ODD_NB = ODD_IN // LANES

TM = 256
OUT_TM = 768
PROJ_TN = 512
VMEM_LIMIT = 48 * 1024 * 1024


def _cparams(n_axes):
    return pltpu.CompilerParams(dimension_semantics=("parallel",) * n_axes, vmem_limit_bytes=VMEM_LIMIT)


def _silu(x):
    return x * jax.nn.sigmoid(x)


def _mod_kernel(c_ref, w_ref, b_ref, o_ref):
    a = _silu(c_ref[...])
    o_ref[0] = jnp.dot(a, w_ref[0], preferred_element_type=F32, precision=lax.Precision.HIGHEST) + b_ref[0]


def _modulation(c_aug, ada_w, ada_b):
    rows = c_aug.shape[0]
    nt = 3 * D_MODEL // D_MODEL
    return pl.pallas_call(
        _mod_kernel,
        out_shape=jax.ShapeDtypeStruct((DEPTH, rows, 3 * D_MODEL), F32),
        grid=(DEPTH, nt),
        in_specs=[
            pl.BlockSpec((rows, D_MODEL), lambda l, n: (0, 0)),
            pl.BlockSpec((1, D_MODEL, D_MODEL), lambda l, n: (l, 0, n)),
            pl.BlockSpec((1, 1, D_MODEL), lambda l, n: (l, 0, n)),
        ],
        out_specs=pl.BlockSpec((1, rows, D_MODEL), lambda l, n: (l, 0, n)),
        compiler_params=_cparams(2),
        name="adaln_modulation",
    )(c_aug, ada_w, ada_b.reshape(DEPTH, 1, 3 * D_MODEL))


def _lam_kernel(lp_ref, o_ref, *, lam_inits):
    for i, lam_init in enumerate(lam_inits):
        lp = lp_ref[i]
        a = jnp.sum(lp[0:1] * lp[1:2], axis=-1, keepdims=True)
        b = jnp.sum(lp[2:3] * lp[3:4], axis=-1, keepdims=True)
        lam = jnp.exp(a) - jnp.exp(b) + lam_init
        o_ref[i] = jnp.broadcast_to(lam, (1, LANES))


def _diff_lambdas(diff_lambda, lam_inits):
    n = diff_lambda.shape[0]
    return pl.pallas_call(
        functools.partial(_lam_kernel, lam_inits=lam_inits),
        out_shape=jax.ShapeDtypeStruct((n, 1, LANES), F32),
        name="diff_lambda",
    )(diff_lambda)


def _proj_first_kernel(ctx_ref, x_ref, *rest, **static):
    *rest, o_ref, xx_ref = rest
    x = jnp.where(pl.program_id(1) == 0, ctx_ref[0], x_ref[0])
    xx_ref[0] = x
    _proj_body(x, *rest, o_ref, **static)


def _proj_kernel(x_ref, *rest, **static):
    _proj_body(x_ref[0], *rest, **static)


def _proj_body(x, mod_ref, g_ref, w_ref, c_ref, s1_ref, s2_ref, o_ref, *, n_rope, n_q, rot, k_scale):
    m = mod_ref[0, 0]
    y = x * lax.rsqrt(jnp.mean(x * x, axis=-1, keepdims=True) + EPS) * g_ref[...]
    h = (y * (1.0 + m[1:2]) + m[0:1]).astype(BF16)
    cos, sin_up, sin_dn = c_ref[...], s1_ref[...], s2_ref[...]
    per = PROJ_TN // LANES
    for n in range(w_ref.shape[1] // PROJ_TN):
        r = jnp.dot(h, w_ref[:, n * PROJ_TN:(n + 1) * PROJ_TN], preferred_element_type=F32)
        for cb in range(per):
            blk = n * per + cb
            v = r[:, cb * LANES:(cb + 1) * LANES]
            if blk < n_rope:
                v = v * cos + pltpu.roll(v, LANES - rot, 1) * sin_up + pltpu.roll(v, rot, 1) * sin_dn
                if blk >= n_q:
                    v = v * k_scale
            o_ref[0, blk] = v.astype(BF16)


def _project(xx, modsel, g_pre, w_bf16, tables, *, ctx=None, **static):
    b, l, d = xx.shape
    first = ctx is not None
    if first:
        l += ctx.shape[1]
    n_in = w_bf16.shape[1]
    nb = n_in // LANES
    tab_spec = pl.BlockSpec((TM, LANES), lambda i, j: (j, 0))
    common_specs = [
        pl.BlockSpec((1, 1, 3, d), lambda i, j: (i, jnp.minimum(j, 1), 0, 0)),
        pl.BlockSpec((1, d), lambda i, j: (0, 0)),
        pl.BlockSpec((d, n_in), lambda i, j: (0, 0)),
        tab_spec, tab_spec, tab_spec,
    ]
    proj_shape = jax.ShapeDtypeStruct((b, nb, l, LANES), BF16)
    proj_spec = pl.BlockSpec((1, nb, TM, LANES), lambda i, j: (i, 0, j, 0))
    if first:
        src = (ctx, xx)
        src_specs = [pl.BlockSpec((1, TM, d), lambda i, j: (i, 0, 0)),
                     pl.BlockSpec((1, TM, d), lambda i, j: (i, jnp.maximum(j - 1, 0), 0))]
        out_shape = (proj_shape, jax.ShapeDtypeStruct((b, l, d), F32))
        out_specs = (proj_spec, pl.BlockSpec((1, TM, d), lambda i, j: (i, j, 0)))
    else:
        src = (xx,)
        src_specs = [pl.BlockSpec((1, TM, d), lambda i, j: (i, j, 0))]
        out_shape, out_specs = proj_shape, proj_spec
    return pl.pallas_call(
        functools.partial(_proj_first_kernel if first else _proj_kernel, **static),
        out_shape=out_shape,
        grid=(b, l // TM),
        in_specs=src_specs + common_specs,
        out_specs=out_specs,
        compiler_params=_cparams(2),
        name="input_projection",
    )(*src, modsel, g_pre, w_bf16, *tables)


def _out_kernel(za_ref, zb_ref, w_ref, x_ref, mod_ref, g_ref, o_ref, *, first_row, n_ctx):
    half = za_ref.shape[-1]
    tm = x_ref.shape[1]
    y = jnp.dot(za_ref[0], w_ref[0:half], preferred_element_type=F32)
    y = y + jnp.dot(zb_ref[0], w_ref[half:2 * half], preferred_element_type=F32)
    r = y * lax.rsqrt(jnp.mean(y * y, axis=-1, keepdims=True) + EPS) * g_ref[...]
    row = first_row + pl.program_id(1) * tm + lax.broadcasted_iota(jnp.int32, (tm, 1), 0)
    gate = jnp.where(row < n_ctx, mod_ref[0, 0][2:3], mod_ref[0, 1][2:3])
    o_ref[0] = x_ref[0] + gate * r


def _out_project(za, zb, zb_col, w_bf16, xx, modsel, g_post, *, n_ctx, latent_only):
    b, l, d = xx.shape
    half = d // 2
    tm = n_ctx if latent_only else OUT_TM
    skip = 1 if latent_only else 0
    z_skip = skip - (l - za.shape[1]) // tm
    n_blocks = l // tm - skip
    out_rows = n_blocks * tm
    return pl.pallas_call(
        functools.partial(_out_kernel, first_row=skip * tm, n_ctx=n_ctx),
        out_shape=jax.ShapeDtypeStruct((b, out_rows, d), F32),
        grid=(b, n_blocks),
        in_specs=[
            pl.BlockSpec((1, tm, half), lambda i, j: (i, j + z_skip, 0)),
            pl.BlockSpec((1, tm, half), lambda i, j: (i, j + z_skip, zb_col)),
            pl.BlockSpec((d, d), lambda i, j: (0, 0)),
            pl.BlockSpec((1, tm, d), lambda i, j: (i, j + skip, 0)),
            pl.BlockSpec((1, 2, 3, d), lambda i, j: (i, 0, 0, 0)),
            pl.BlockSpec((1, d), lambda i, j: (0, 0)),
        ],
        out_specs=pl.BlockSpec((1, tm, d), lambda i, j: (i, j, 0)),
        input_output_aliases={} if latent_only else {3: 0},
        compiler_params=_cparams(2),
        name="output_projection",
    )(za, zb, w_bf16, xx, modsel, g_post)


def _ret_kernel(dl_ref, q_ref, k_ref, v_ref, g_ref, o_ref, *, n_ctx_chunks):
    o_ref[0] = jnp.concatenate(
        [_ret_head(dl_ref, q_ref, k_ref, v_ref, g_ref, hh, n_ctx_chunks) for hh in range(q_ref.shape[1])], axis=1)


def _ret_head(dl_ref, q_ref, k_ref, v_ref, g_ref, hh, n_ctx_chunks):
    c = RET_CHUNK
    n_chunks = q_ref.shape[2] // c
    lg = -jnp.exp(dl_ref[hh])
    lgf, lgb = lg[0:1], lg[1:2]
    ri = lax.broadcasted_iota(jnp.int32, (c, c), 0).astype(F32)
    ci = lax.broadcasted_iota(jnp.int32, (c, c), 1).astype(F32)
    rel = ri - ci
    decay = jnp.where(rel >= 0.0, jnp.exp(lgf * jnp.maximum(rel, 0.0)), jnp.exp(lgb * jnp.maximum(-rel, 0.0)))
    q_dec_f = jnp.exp(lgf * (ri + 1.0))
    q_dec_b = jnp.exp(lgb * (c - ri))
    k_dec_f = jnp.exp(lgf * (c - 1.0 - ri))
    k_dec_b = jnp.exp(lgb * ri)
    c_dec_f = jnp.exp(lgf * c)
    c_dec_b = jnp.exp(lgb * c)

    def chunk(ref, i):
        return ref[0, hh, i * c:(i + 1) * c, :]

    updates = []
    for i in range(n_chunks):
        kc = chunk(k_ref, i).astype(F32)
        kk = jnp.concatenate([kc * k_dec_f, kc * k_dec_b], axis=1).astype(BF16)
        updates.append(lax.dot_general(kk, chunk(v_ref, i), (((0,), (0,)), ((), ())), preferred_element_type=F32))

    before = [None] * n_chunks
    state = jnp.zeros((c, LANES), F32)
    for i in range(n_chunks):
        before[i] = [state]
        state = state * c_dec_f + updates[i][0:c]
    order = list(range(n_ctx_chunks - 1, -1, -1)) + list(range(n_chunks - 1, n_ctx_chunks - 1, -1))
    state = jnp.zeros((c, LANES), F32)
    for i in order:
        before[i].append(state)
        state = state * c_dec_b + updates[i][c:2 * c]

    masked = [(lax.dot_general(chunk(q_ref, i), chunk(k_ref, i), (((1,), (1,)), ((), ())),
                               preferred_element_type=F32) * decay).astype(BF16) for i in range(n_chunks)]
    crosses = []
    for i in range(n_chunks):
        qf = chunk(q_ref, i).astype(F32)
        qq = jnp.concatenate([qf * q_dec_f, qf * q_dec_b], axis=1).astype(BF16)
        states = jnp.concatenate(before[i], axis=0).astype(BF16)
        crosses.append(jnp.dot(qq, states, preferred_element_type=F32))
    o = jnp.concatenate([jnp.dot(masked[i], chunk(v_ref, i), preferred_element_type=F32) + crosses[i]
                         for i in range(n_chunks)], axis=0)
    mu = jnp.mean(o, axis=-1, keepdims=True)
    dev = o - mu
    var = jnp.mean(dev * dev, axis=-1, keepdims=True)
    g = g_ref[0, hh].astype(F32)
    return (dev * lax.rsqrt(var + EPS) * _silu(g)).astype(BF16)


def _retention(proj, decay_logit_lanes, n_ctx):
    b, _, l, _ = proj.shape
    hp = RET_HEADS_PER_STEP
    ng = RET_HEADS // hp

    def col(kind):
        return pl.BlockSpec((1, hp, l, LANES), lambda i, h: (i, kind * ng + h, 0, 0))

    return pl.pallas_call(
        functools.partial(_ret_kernel, n_ctx_chunks=n_ctx // RET_CHUNK),
        out_shape=jax.ShapeDtypeStruct((b, l, RET_HEADS * LANES), BF16),
        grid=(b, ng),
        in_specs=[
            pl.BlockSpec((hp, 2, LANES), lambda i, h: (h, 0, 0)),
            col(0), col(1), col(2), col(3),
        ],
        out_specs=pl.BlockSpec((1, l, hp * LANES), lambda i, h: (i, 0, h)),
        compiler_params=_cparams(2),
        name="retention",
    )(decay_logit_lanes, proj, proj, proj, proj)


def _pool_kernel(p_ref, g_ref, w_ref, sc_ref, o_ref, pad_scr, *, segments):
    for grp, win in enumerate(POOL_WINDOWS):
        below = win // 2
        above = win - below - 1
        base = 0
        for start, length in segments:
            zeros = jnp.zeros((POOL_PAD, LANES), F32)
            pad_scr[base:base + POOL_PAD] = zeros
            pad_scr[base + POOL_PAD:base + POOL_PAD + length] = p_ref[0, grp, start:start + length, :].astype(F32)
            pad_scr[base + POOL_PAD + length:base + 2 * POOL_PAD + length] = zeros
            base += length + 2 * POOL_PAD
        base = 0
        for start, length in segments:
            first = base + POOL_PAD
            total = pad_scr[first - below:first - below + length]
            for off in range(-below + 1, above + 1):
                total = total + pad_scr[first + off:first + off + length]
            pos = lax.broadcasted_iota(jnp.int32, (length, LANES), 0)
            lo = jnp.maximum(pos - below, 0)
            hi = jnp.minimum(pos + above, length - 1)
            mean = total / (hi - lo + 1).astype(F32)
            d = mean - pad_scr[first:first + length]
            y = jnp.dot(d.astype(BF16), w_ref[grp], preferred_element_type=F32) * sc_ref[grp]
            g = g_ref[0, grp, start:start + length, :].astype(F32)
            o_ref[0, start:start + length, grp * LANES:(grp + 1) * LANES] = (y * _silu(g)).astype(BF16)
            base += length + 2 * POOL_PAD


def _pooling(proj, pool_w_bf16, pool_scale, segments):
    b, _, l, _ = proj.shape
    g = POOL_GROUPS
    pad_rows = l + 2 * POOL_PAD * len(segments)
    return pl.pallas_call(
        functools.partial(_pool_kernel, segments=segments),
        out_shape=jax.ShapeDtypeStruct((b, l, g * LANES), BF16),
        grid=(b,),
        in_specs=[
            pl.BlockSpec((1, g, l, LANES), lambda i: (i, 4, 0, 0)),
            pl.BlockSpec((1, g, l, LANES), lambda i: (i, 5, 0, 0)),
            pl.BlockSpec((g, LANES, LANES), lambda i: (0, 0, 0)),
            pl.BlockSpec((g, 1, LANES), lambda i: (0, 0, 0)),
        ],
        out_specs=pl.BlockSpec((1, l, g * LANES), lambda i: (i, 0, 0)),
        scratch_shapes=[pltpu.VMEM((pad_rows, LANES), F32)],
        compiler_params=_cparams(1),
        name="pool_mix",
    )(proj, proj, pool_w_bf16, pool_scale.reshape(g, 1, LANES))


def _attn_kernel(lam_ref, sub_ref, q_ref, k_ref, v_ref, g_ref, o_ref, *, n_ctx, out_scale, skip_ctx):
    lam = lam_ref[...][:, 0:1]
    lane = lax.broadcasted_iota(jnp.int32, (q_ref.shape[2], LANES), 1)

    def attend(n_keys):
        n_heads, tq = q_ref.shape[1], q_ref.shape[2]
        chunk = min(ATTN_KEY_CHUNK, n_keys)
        n_chunks = n_keys // chunk

        def stacked_q(hh):
            q = q_ref[0, hh] * (DIFF_DH ** -0.5 * LOG2_E)
            zero = jnp.zeros_like(q)
            return jnp.concatenate([jnp.where(lane < DIFF_DH, q, zero), jnp.where(lane >= DIFF_DH, q, zero)], axis=0)

        def keys(t):
            return slice(t * chunk, (t + 1) * chunk)

        outs = []
        scored, exped = None, None
        for i in range(n_heads + 2):
            qs = stacked_q(i) if i < n_heads else None
            s_chunks, m_acc, p_chunks, l_acc, o_t = [], None, [], None, None
            if exped is not None:
                l0, l1 = exped[1][:, 0:tq], exped[1][:, tq:2 * tq]
                ratio = lam * l0 / l1
            for t in range(n_chunks):
                if qs is not None:
                    s_t = lax.dot_general(k_ref[0, i, keys(t), :], qs, (((1,), (1,)), ((), ())),
                                          preferred_element_type=F32)
                    s_chunks.append(s_t)
                    m_t = jnp.max(s_t, axis=0, keepdims=True)
                    m_acc = m_t if m_acc is None else jnp.maximum(m_acc, m_t)
                if scored is not None:
                    p_t = jnp.exp2(scored[0][t] - scored[1])
                    p_chunks.append(p_t)
                    l_t = jnp.sum(p_t, axis=0, keepdims=True)
                    l_acc = l_t if l_acc is None else l_acc + l_t
                if exped is not None:
                    a_t = (exped[0][t][:, 0:tq] - exped[0][t][:, tq:2 * tq] * ratio).astype(BF16)
                    part = lax.dot_general(v_ref[0, i - 2, keys(t), :], a_t, (((0,), (0,)), ((), ())),
                                           preferred_element_type=F32)
                    o_t = part if o_t is None else o_t + part
            if exped is not None:
                o_t = o_t * (1.0 / l0)
                o_t = o_t * lax.rsqrt(jnp.mean(o_t * o_t, axis=0, keepdims=True) + EPS)
                o = o_t.T * sub_ref[...] * out_scale
                g = g_ref[0, i - 2].astype(F32)
                outs.append((o * _silu(g)).astype(BF16))
            exped = (p_chunks, l_acc) if scored is not None else None
            scored = (s_chunks, m_acc) if qs is not None else None
        o_ref[0] = jnp.concatenate(outs, axis=1)

    if skip_ctx:
        attend(k_ref.shape[2])
        return
    j = pl.program_id(2)

    @pl.when(j == 0)
    def _():
        attend(n_ctx)

    @pl.when(j > 0)
    def _():
        attend(k_ref.shape[2])


def _diff_attention(proj, lam, subln, *, n_ctx, out_scale, skip_ctx):
    b, _, l, _ = proj.shape
    hp = ATTN_HEADS_PER_STEP
    ng = DIFF_HEADS // hp
    skip = 1 if skip_ctx else 0
    kern = functools.partial(_attn_kernel, n_ctx=n_ctx, out_scale=out_scale, skip_ctx=skip_ctx)
    return pl.pallas_call(
        kern,
        out_shape=jax.ShapeDtypeStruct((b, l - skip * TM, DIFF_HEADS * LANES), BF16),
        grid=(b, ng, l // TM - skip),
        in_specs=[
            pl.BlockSpec((1, LANES), lambda i, hh, j: (0, 0)),
            pl.BlockSpec((1, LANES), lambda i, hh, j: (0, 0)),
            pl.BlockSpec((1, hp, TM, LANES), lambda i, hh, j: (i, hh, j + skip, 0)),
            pl.BlockSpec((1, hp, l, LANES), lambda i, hh, j: (i, ng + hh, 0, 0)),
            pl.BlockSpec((1, hp, l, LANES), lambda i, hh, j: (i, 2 * ng + hh, 0, 0)),
            pl.BlockSpec((1, hp, TM, LANES), lambda i, hh, j: (i, 3 * ng + hh, j + skip, 0)),
        ],
        out_specs=pl.BlockSpec((1, TM, hp * LANES), lambda i, hh, j: (i, j, hh)),
        compiler_params=_cparams(3),
        name="diff_attention",
    )(lam, subln.reshape(1, LANES), proj, proj, proj, proj)


def _rope_tables(n_ctx, seq, axis_dim):
    t = jnp.arange(seq)
    rows = (t // GRID_W).astype(F32)
    cols = (t % GRID_W).astype(F32)
    freqs = ROPE_BASE ** (-jnp.arange(axis_dim // 2, dtype=F32) * 2.0 / axis_dim)
    lane = jnp.arange(LANES)
    use_cols = (lane % (2 * axis_dim)) // axis_dim == 1
    within = lane % axis_dim
    first = within < axis_dim // 2
    f = freqs[within % (axis_dim // 2)]
    pos = jnp.where(use_cols[None, :], cols[:, None], rows[:, None])
    ang = pos * f[None, :]
    cos, sin = jnp.cos(ang), jnp.sin(ang)
    sin_up = jnp.where(first[None, :], -sin, 0.0)
    sin_dn = jnp.where(first[None, :], 0.0, sin)
    pad = lambda a, fill: jnp.concatenate([jnp.full((n_ctx, LANES), fill, F32), a.astype(F32)], axis=0)
    return pad(cos, 1.0), pad(sin_up, 0.0), pad(sin_dn, 0.0)


def kernel(x, c, ctx, c_ctx, ada_w, ada_b, norm_pre, norm_post, ev_w_in, ev_w_out, ret_decay_logit, pool_w,
           pool_scale, od_w_in, od_w_out, diff_lambda, diff_subln):
    b, seq, d = x.shape
    n_ctx = ctx.shape[1]
    assert n_ctx == TM and seq % TM == 0 and (n_ctx + seq) % OUT_TM == 0 and d == D_MODEL

    xx = x

    mod_rows = ((b + 1 + 7) // 8) * 8
    c_aug = jnp.zeros((mod_rows, d), F32).at[:b].set(c).at[b].set(c_ctx)
    mod = _modulation(c_aug, ada_w, ada_b)

    lam_inits = tuple(0.8 - 0.6 * math.exp(-0.3 * l) for l in range(1, DEPTH, 2))
    lams = _diff_lambdas(diff_lambda, lam_inits)

    even_tabs = _rope_tables(n_ctx, seq, RET_DK // 2)
    odd_tabs = _rope_tables(n_ctx, seq, DIFF_DH // 2)

    for l in range(DEPTH):
        last = l == DEPTH - 1
        i = l // 2
        m = mod[l].reshape(mod_rows, 3, d)
        modsel = jnp.stack([jnp.broadcast_to(m[b], (b, 3, d)), m[:b]], axis=1)
        g_pre = norm_pre[l].reshape(1, d)
        g_post = norm_post[l].reshape(1, d)
        if l % 2 == 0:
            proj = _project(xx, modsel, g_pre, ev_w_in[i].astype(BF16), even_tabs, ctx=ctx if l == 0 else None,
                            n_rope=2 * RET_HEADS, n_q=RET_HEADS, rot=RET_DK // 4, k_scale=RET_DK ** -0.5)
            if l == 0:
                proj, xx = proj
            dl = jnp.broadcast_to(ret_decay_logit[i].T[:, :, None], (RET_HEADS, 2, LANES))
            z_ret = _retention(proj, dl, n_ctx)
            z_pool = _pooling(proj, pool_w[i].astype(BF16), pool_scale[i], ((0, n_ctx), (n_ctx, seq)))
            xx = _out_project(z_ret, z_pool, 0, ev_w_out[i].astype(BF16), xx, modsel, g_post, n_ctx=n_ctx,
                              latent_only=last)
        else:
            proj = _project(xx, modsel, g_pre, od_w_in[i].astype(BF16), odd_tabs,
                            n_rope=2 * DIFF_HEADS, n_q=2 * DIFF_HEADS, rot=DIFF_DH // 4, k_scale=1.0)
            z = _diff_attention(proj, lams[i], diff_subln[i], n_ctx=n_ctx, out_scale=1.0 - lam_inits[i],
                                skip_ctx=last)
            xx = _out_project(z, z, 1, od_w_out[i].astype(BF16), xx, modsel, g_post, n_ctx=n_ctx, latent_only=last)
    return xx
```

```python
import functools
import math

import jax
import jax.numpy as jnp
from jax import lax
from jax.experimental import pallas as pl
from jax.experimental.pallas import tpu as pltpu

F32 = jnp.float32
BF16 = jnp.bfloat16

D_MODEL = 1024
DEPTH = 4
GRID_W = 64
EPS = 1e-6
ROPE_BASE = 10000.0
LANES = 128

RET_HEADS = 4
RET_DK = 128
RET_CHUNK = 128
RET_HEADS_PER_STEP = 2
POOL_GROUPS = 4
POOL_WINDOWS = (2, 4, 8, 16)
POOL_PAD = 8
EVEN_IN = 3072
EVEN_NB = EVEN_IN // LANES

DIFF_HEADS = 8
DIFF_DH = 64
ATTN_HEADS_PER_STEP = 8
ATTN_KEY_CHUNK = 768
LOG2_E = 1.4426950408889634
ODD_IN = 4096
ODD_NB = ODD_IN // LANES

TM = 256
OUT_TM = 768
PROJ_WIDE_TM = 768
PROJ_TN = 512
VMEM_LIMIT = 48 * 1024 * 1024


def _cparams(n_axes):
    return pltpu.CompilerParams(dimension_semantics=("parallel",) * n_axes, vmem_limit_bytes=VMEM_LIMIT)


def _silu(x):
    return x * jax.nn.sigmoid(x)


def _mod_kernel(c_ref, w_ref, b_ref, o_ref):
    a = _silu(c_ref[...])
    o_ref[0] = jnp.dot(a, w_ref[0], preferred_element_type=F32, precision=lax.Precision.HIGHEST) + b_ref[0]


def _modulation(c_aug, ada_w, ada_b):
    rows = c_aug.shape[0]
    nt = 3 * D_MODEL // D_MODEL
    return pl.pallas_call(
        _mod_kernel,
        out_shape=jax.ShapeDtypeStruct((DEPTH, rows, 3 * D_MODEL), F32),
        grid=(DEPTH, nt),
        in_specs=[
            pl.BlockSpec((rows, D_MODEL), lambda l, n: (0, 0)),
            pl.BlockSpec((1, D_MODEL, D_MODEL), lambda l, n: (l, 0, n)),
            pl.BlockSpec((1, 1, D_MODEL), lambda l, n: (l, 0, n)),
        ],
        out_specs=pl.BlockSpec((1, rows, D_MODEL), lambda l, n: (l, 0, n)),
        compiler_params=_cparams(2),
        name="adaln_modulation",
    )(c_aug, ada_w, ada_b.reshape(DEPTH, 1, 3 * D_MODEL))


def _lam_kernel(lp_ref, o_ref, *, lam_inits):
    for i, lam_init in enumerate(lam_inits):
        lp = lp_ref[i]
        a = jnp.sum(lp[0:1] * lp[1:2], axis=-1, keepdims=True)
        b = jnp.sum(lp[2:3] * lp[3:4], axis=-1, keepdims=True)
        lam = jnp.exp(a) - jnp.exp(b) + lam_init
        o_ref[i] = jnp.broadcast_to(lam, (1, LANES))


def _diff_lambdas(diff_lambda, lam_inits):
    n = diff_lambda.shape[0]
    return pl.pallas_call(
        functools.partial(_lam_kernel, lam_inits=lam_inits),
        out_shape=jax.ShapeDtypeStruct((n, 1, LANES), F32),
        name="diff_lambda",
    )(diff_lambda)


def _proj_first_kernel(ctx_ref, x_ref, *rest, **static):
    *rest, o_ref, xx_ref = rest
    x = jnp.where(pl.program_id(1) == 0, ctx_ref[0], x_ref[0])
    xx_ref[0] = x
    _proj_body(x, *rest, o_ref, **static)


def _proj_wide_kernel(x_ref, mod_ref, g_ref, w_ref, c_ref, s1_ref, s2_ref, o_ref, *, sub_rows, **static):
    n_sub = x_ref.shape[1] // sub_rows
    at_start = pl.program_id(1) == 0

    def modulated(s):
        m = mod_ref[0, 1]
        if s == 0:
            m = jnp.where(at_start, mod_ref[0, 0], m)
        return _modulated_norm(x_ref[0, s * sub_rows:(s + 1) * sub_rows], m, g_ref)

    h = modulated(0)
    for s in range(n_sub):
        nxt = modulated(s + 1) if s + 1 < n_sub else None
        _proj_rows(h, slice(s * sub_rows, (s + 1) * sub_rows), w_ref, c_ref, s1_ref, s2_ref, o_ref, **static)
        h = nxt


def _proj_body(x, mod_ref, g_ref, w_ref, c_ref, s1_ref, s2_ref, o_ref, **static):
    h = _modulated_norm(x, mod_ref[0, 0], g_ref)
    _proj_rows(h, slice(0, x.shape[0]), w_ref, c_ref, s1_ref, s2_ref, o_ref, **static)


def _modulated_norm(x, m, g_ref):
    y = x * lax.rsqrt(jnp.mean(x * x, axis=-1, keepdims=True) + EPS) * g_ref[...]
    return (y * (1.0 + m[1:2]) + m[0:1]).astype(BF16)


def _proj_rows(h, rows, w_ref, c_ref, s1_ref, s2_ref, o_ref, *, n_rope, n_q, rot, k_scale):
    cos, sin_up, sin_dn = c_ref[rows], s1_ref[rows], s2_ref[rows]
    per = PROJ_TN // LANES
    for n in range(w_ref.shape[1] // PROJ_TN):
        r = jnp.dot(h, w_ref[:, n * PROJ_TN:(n + 1) * PROJ_TN], preferred_element_type=F32)
        for cb in range(per):
            blk = n * per + cb
            v = r[:, cb * LANES:(cb + 1) * LANES]
            if blk < n_rope:
                v = v * cos + pltpu.roll(v, LANES - rot, 1) * sin_up + pltpu.roll(v, rot, 1) * sin_dn
                if blk >= n_q:
                    v = v * k_scale
            o_ref[0, blk, rows, :] = v.astype(BF16)


def _project(xx, modsel, g_pre, w_bf16, tables, *, ctx=None, **static):
    b, l, d = xx.shape
    first = ctx is not None
    if first:
        l += ctx.shape[1]
    n_in = w_bf16.shape[1]
    nb = n_in // LANES
    tm = TM if first else PROJ_WIDE_TM
    tab_spec = pl.BlockSpec((tm, LANES), lambda i, j: (j, 0))
    mod_spec = (pl.BlockSpec((1, 1, 3, d), lambda i, j: (i, jnp.minimum(j, 1), 0, 0)) if first
                else pl.BlockSpec((1, 2, 3, d), lambda i, j: (i, 0, 0, 0)))
    common_specs = [
        mod_spec,
        pl.BlockSpec((1, d), lambda i, j: (0, 0)),
        pl.BlockSpec((d, n_in), lambda i, j: (0, 0)),
        tab_spec, tab_spec, tab_spec,
    ]
    proj_shape = jax.ShapeDtypeStruct((b, nb, l, LANES), BF16)
    proj_spec = pl.BlockSpec((1, nb, tm, LANES), lambda i, j: (i, 0, j, 0))
    if first:
        kern = functools.partial(_proj_first_kernel, **static)
        src = (ctx, xx)
        src_specs = [pl.BlockSpec((1, tm, d), lambda i, j: (i, 0, 0)),
                     pl.BlockSpec((1, tm, d), lambda i, j: (i, jnp.maximum(j - 1, 0), 0))]
        out_shape = (proj_shape, jax.ShapeDtypeStruct((b, l, d), F32))
        out_specs = (proj_spec, pl.BlockSpec((1, tm, d), lambda i, j: (i, j, 0)))
    else:
        kern = functools.partial(_proj_wide_kernel, sub_rows=TM, **static)
        src = (xx,)
        src_specs = [pl.BlockSpec((1, tm, d), lambda i, j: (i, j, 0))]
        out_shape, out_specs = proj_shape, proj_spec
    return pl.pallas_call(
        kern,
        out_shape=out_shape,
        grid=(b, l // tm),
        in_specs=src_specs + common_specs,
        out_specs=out_specs,
        compiler_params=_cparams(2),
        name="input_projection",
    )(*src, modsel, g_pre, w_bf16, *tables)


def _out_kernel(za_ref, zb_ref, w_ref, x_ref, mod_ref, g_ref, o_ref, *, first_row, n_ctx):
    half = za_ref.shape[-1]
    tm = x_ref.shape[1]
    y = jnp.dot(za_ref[0], w_ref[0:half], preferred_element_type=F32)
    y = y + jnp.dot(zb_ref[0], w_ref[half:2 * half], preferred_element_type=F32)
    r = y * lax.rsqrt(jnp.mean(y * y, axis=-1, keepdims=True) + EPS) * g_ref[...]
    row = first_row + pl.program_id(1) * tm + lax.broadcasted_iota(jnp.int32, (tm, 1), 0)
    gate = jnp.where(row < n_ctx, mod_ref[0, 0][2:3], mod_ref[0, 1][2:3])
    o_ref[0] = x_ref[0] + gate * r


def _out_project(za, zb, zb_col, w_bf16, xx, modsel, g_post, *, n_ctx, latent_only):
    b, l, d = xx.shape
    half = d // 2
    tm = n_ctx if latent_only else OUT_TM
    skip = 1 if latent_only else 0
    z_skip = skip - (l - za.shape[1]) // tm
    n_blocks = l // tm - skip
    out_rows = n_blocks * tm
    return pl.pallas_call(
        functools.partial(_out_kernel, first_row=skip * tm, n_ctx=n_ctx),
        out_shape=jax.ShapeDtypeStruct((b, out_rows, d), F32),
        grid=(b, n_blocks),
        in_specs=[
            pl.BlockSpec((1, tm, half), lambda i, j: (i, j + z_skip, 0)),
            pl.BlockSpec((1, tm, half), lambda i, j: (i, j + z_skip, zb_col)),
            pl.BlockSpec((d, d), lambda i, j: (0, 0)),
            pl.BlockSpec((1, tm, d), lambda i, j: (i, j + skip, 0)),
            pl.BlockSpec((1, 2, 3, d), lambda i, j: (i, 0, 0, 0)),
            pl.BlockSpec((1, d), lambda i, j: (0, 0)),
        ],
        out_specs=pl.BlockSpec((1, tm, d), lambda i, j: (i, j, 0)),
        input_output_aliases={} if latent_only else {3: 0},
        compiler_params=_cparams(2),
        name="output_projection",
    )(za, zb, w_bf16, xx, modsel, g_post)


def _ret_kernel(dl_ref, q_ref, k_ref, v_ref, g_ref, o_ref, *, n_ctx_chunks):
    o_ref[0] = jnp.concatenate(
        [_ret_head(dl_ref, q_ref, k_ref, v_ref, g_ref, hh, n_ctx_chunks) for hh in range(q_ref.shape[1])], axis=1)


def _ret_head(dl_ref, q_ref, k_ref, v_ref, g_ref, hh, n_ctx_chunks):
    c = RET_CHUNK
    n_chunks = q_ref.shape[2] // c
    lg = -jnp.exp(dl_ref[hh])
    lgf, lgb = lg[0:1], lg[1:2]
    ri = lax.broadcasted_iota(jnp.int32, (c, c), 0).astype(F32)
    ci = lax.broadcasted_iota(jnp.int32, (c, c), 1).astype(F32)
    rel = ri - ci
    decay = jnp.where(rel >= 0.0, jnp.exp(lgf * jnp.maximum(rel, 0.0)), jnp.exp(lgb * jnp.maximum(-rel, 0.0)))
    q_dec_f = jnp.exp(lgf * (ri + 1.0))
    q_dec_b = jnp.exp(lgb * (c - ri))
    k_dec_f = jnp.exp(lgf * (c - 1.0 - ri))
    k_dec_b = jnp.exp(lgb * ri)
    c_dec_f = jnp.exp(lgf * c)
    c_dec_b = jnp.exp(lgb * c)

    def chunk(ref, i):
        return ref[0, hh, i * c:(i + 1) * c, :]

    updates = []
    for i in range(n_chunks):
        kc = chunk(k_ref, i).astype(F32)
        kk = jnp.concatenate([kc * k_dec_f, kc * k_dec_b], axis=1).astype(BF16)
        updates.append(lax.dot_general(kk, chunk(v_ref, i), (((0,), (0,)), ((), ())), preferred_element_type=F32))

    before = [None] * n_chunks
    state = jnp.zeros((c, LANES), F32)
    for i in range(n_chunks):
        before[i] = [state]
        state = state * c_dec_f + updates[i][0:c]
    order = list(range(n_ctx_chunks - 1, -1, -1)) + list(range(n_chunks - 1, n_ctx_chunks - 1, -1))
    state = jnp.zeros((c, LANES), F32)
    for i in order:
        before[i].append(state)
        state = state * c_dec_b + updates[i][c:2 * c]

    masked = [(lax.dot_general(chunk(q_ref, i), chunk(k_ref, i), (((1,), (1,)), ((), ())),
                               preferred_element_type=F32) * decay).astype(BF16) for i in range(n_chunks)]
    crosses = []
    for i in range(n_chunks):
        qf = chunk(q_ref, i).astype(F32)
        qq = jnp.concatenate([qf * q_dec_f, qf * q_dec_b], axis=1).astype(BF16)
        states = jnp.concatenate(before[i], axis=0).astype(BF16)
        crosses.append(jnp.dot(qq, states, preferred_element_type=F32))
    o = jnp.concatenate([jnp.dot(masked[i], chunk(v_ref, i), preferred_element_type=F32) + crosses[i]
                         for i in range(n_chunks)], axis=0)
    mu = jnp.mean(o, axis=-1, keepdims=True)
    dev = o - mu
    var = jnp.mean(dev * dev, axis=-1, keepdims=True)
    g = g_ref[0, hh].astype(F32)
    return (dev * lax.rsqrt(var + EPS) * _silu(g)).astype(BF16)


def _retention(proj, decay_logit_lanes, n_ctx):
    b, _, l, _ = proj.shape
    hp = RET_HEADS_PER_STEP
    ng = RET_HEADS // hp

    def col(kind):
        return pl.BlockSpec((1, hp, l, LANES), lambda i, h: (i, kind * ng + h, 0, 0))

    return pl.pallas_call(
        functools.partial(_ret_kernel, n_ctx_chunks=n_ctx // RET_CHUNK),
        out_shape=jax.ShapeDtypeStruct((b, l, RET_HEADS * LANES), BF16),
        grid=(b, ng),
        in_specs=[
            pl.BlockSpec((hp, 2, LANES), lambda i, h: (h, 0, 0)),
            col(0), col(1), col(2), col(3),
        ],
        out_specs=pl.BlockSpec((1, l, hp * LANES), lambda i, h: (i, 0, h)),
        compiler_params=_cparams(2),
        name="retention",
    )(decay_logit_lanes, proj, proj, proj, proj)


def _pool_kernel(p_ref, g_ref, w_ref, sc_ref, o_ref, pad_scr, *, segments):
    for grp, win in enumerate(POOL_WINDOWS):
        below = win // 2
        above = win - below - 1
        base = 0
        for start, length in segments:
            zeros = jnp.zeros((POOL_PAD, LANES), F32)
            pad_scr[base:base + POOL_PAD] = zeros
            pad_scr[base + POOL_PAD:base + POOL_PAD + length] = p_ref[0, grp, start:start + length, :].astype(F32)
            pad_scr[base + POOL_PAD + length:base + 2 * POOL_PAD + length] = zeros
            base += length + 2 * POOL_PAD
        base = 0
        for start, length in segments:
            first = base + POOL_PAD
            total = pad_scr[first - below:first - below + length]
            for off in range(-below + 1, above + 1):
                total = total + pad_scr[first + off:first + off + length]
            pos = lax.broadcasted_iota(jnp.int32, (length, LANES), 0)
            lo = jnp.maximum(pos - below, 0)
            hi = jnp.minimum(pos + above, length - 1)
            mean = total / (hi - lo + 1).astype(F32)
            d = mean - pad_scr[first:first + length]
            y = jnp.dot(d.astype(BF16), w_ref[grp], preferred_element_type=F32) * sc_ref[grp]
            g = g_ref[0, grp, start:start + length, :].astype(F32)
            o_ref[0, start:start + length, grp * LANES:(grp + 1) * LANES] = (y * _silu(g)).astype(BF16)
            base += length + 2 * POOL_PAD


def _pooling(proj, pool_w_bf16, pool_scale, segments):
    b, _, l, _ = proj.shape
    g = POOL_GROUPS
    pad_rows = l + 2 * POOL_PAD * len(segments)
    return pl.pallas_call(
        functools.partial(_pool_kernel, segments=segments),
        out_shape=jax.ShapeDtypeStruct((b, l, g * LANES), BF16),
        grid=(b,),
        in_specs=[
            pl.BlockSpec((1, g, l, LANES), lambda i: (i, 4, 0, 0)),
            pl.BlockSpec((1, g, l, LANES), lambda i: (i, 5, 0, 0)),
            pl.BlockSpec((g, LANES, LANES), lambda i: (0, 0, 0)),
            pl.BlockSpec((g, 1, LANES), lambda i: (0, 0, 0)),
        ],
        out_specs=pl.BlockSpec((1, l, g * LANES), lambda i: (i, 0, 0)),
        scratch_shapes=[pltpu.VMEM((pad_rows, LANES), F32)],
        compiler_params=_cparams(1),
        name="pool_mix",
    )(proj, proj, pool_w_bf16, pool_scale.reshape(g, 1, LANES))


def _attn_kernel(lam_ref, sub_ref, q_ref, k_ref, v_ref, g_ref, o_ref, *, n_ctx, out_scale, skip_ctx):
    lam = lam_ref[...][:, 0:1]
    lane = lax.broadcasted_iota(jnp.int32, (q_ref.shape[2], LANES), 1)

    def attend(n_keys):
        n_heads, tq = q_ref.shape[1], q_ref.shape[2]
        chunk = min(ATTN_KEY_CHUNK, n_keys)
        n_chunks = n_keys // chunk

        def stacked_q(hh):
            q = q_ref[0, hh] * (DIFF_DH ** -0.5 * LOG2_E)
            zero = jnp.zeros_like(q)
            return jnp.concatenate([jnp.where(lane < DIFF_DH, q, zero), jnp.where(lane >= DIFF_DH, q, zero)], axis=0)

        def keys(t):
            return slice(t * chunk, (t + 1) * chunk)

        outs = []
        scored, exped = None, None
        for i in range(n_heads + 2):
            qs = stacked_q(i) if i < n_heads else None
            s_chunks, m_acc, p_chunks, l_acc, o_t = [], None, [], None, None
            if exped is not None:
                l0, l1 = exped[1][:, 0:tq], exped[1][:, tq:2 * tq]
                ratio = lam * l0 / l1
            for t in range(n_chunks):
                if qs is not None:
                    s_t = lax.dot_general(k_ref[0, i, keys(t), :], qs, (((1,), (1,)), ((), ())),
                                          preferred_element_type=F32)
                    s_chunks.append(s_t)
                    m_t = jnp.max(s_t, axis=0, keepdims=True)
                    m_acc = m_t if m_acc is None else jnp.maximum(m_acc, m_t)
                if scored is not None:
                    p_t = jnp.exp2(scored[0][t] - scored[1])
                    p_chunks.append(p_t)
                    l_t = jnp.sum(p_t, axis=0, keepdims=True)
                    l_acc = l_t if l_acc is None else l_acc + l_t
                if exped is not None:
                    a_t = (exped[0][t][:, 0:tq] - exped[0][t][:, tq:2 * tq] * ratio).astype(BF16)
                    part = lax.dot_general(v_ref[0, i - 2, keys(t), :], a_t, (((0,), (0,)), ((), ())),
                                           preferred_element_type=F32)
                    o_t = part if o_t is None else o_t + part
            if exped is not None:
                o_t = o_t * (1.0 / l0)
                o_t = o_t * lax.rsqrt(jnp.mean(o_t * o_t, axis=0, keepdims=True) + EPS)
                o = o_t.T * sub_ref[...] * out_scale
                g = g_ref[0, i - 2].astype(F32)
                outs.append((o * _silu(g)).astype(BF16))
            exped = (p_chunks, l_acc) if scored is not None else None
            scored = (s_chunks, m_acc) if qs is not None else None
        o_ref[0] = jnp.concatenate(outs, axis=1)

    if skip_ctx:
        attend(k_ref.shape[2])
        return
    j = pl.program_id(2)

    @pl.when(j == 0)
    def _():
        attend(n_ctx)

    @pl.when(j > 0)
    def _():
        attend(k_ref.shape[2])


def _diff_attention(proj, lam, subln, *, n_ctx, out_scale, skip_ctx):
    b, _, l, _ = proj.shape
    hp = ATTN_HEADS_PER_STEP
    ng = DIFF_HEADS // hp
    skip = 1 if skip_ctx else 0
    kern = functools.partial(_attn_kernel, n_ctx=n_ctx, out_scale=out_scale, skip_ctx=skip_ctx)
    return pl.pallas_call(
        kern,
        out_shape=jax.ShapeDtypeStruct((b, l - skip * TM, DIFF_HEADS * LANES), BF16),
        grid=(b, ng, l // TM - skip),
        in_specs=[
            pl.BlockSpec((1, LANES), lambda i, hh, j: (0, 0)),
            pl.BlockSpec((1, LANES), lambda i, hh, j: (0, 0)),
            pl.BlockSpec((1, hp, TM, LANES), lambda i, hh, j: (i, hh, j + skip, 0)),
            pl.BlockSpec((1, hp, l, LANES), lambda i, hh, j: (i, ng + hh, 0, 0)),
            pl.BlockSpec((1, hp, l, LANES), lambda i, hh, j: (i, 2 * ng + hh, 0, 0)),
            pl.BlockSpec((1, hp, TM, LANES), lambda i, hh, j: (i, 3 * ng + hh, j + skip, 0)),
        ],
        out_specs=pl.BlockSpec((1, TM, hp * LANES), lambda i, hh, j: (i, j, hh)),
        compiler_params=_cparams(3),
        name="diff_attention",
    )(lam, subln.reshape(1, LANES), proj, proj, proj, proj)


def _rope_tables(n_ctx, seq, axis_dim):
    t = jnp.arange(seq)
    rows = (t // GRID_W).astype(F32)
    cols = (t % GRID_W).astype(F32)
    freqs = ROPE_BASE ** (-jnp.arange(axis_dim // 2, dtype=F32) * 2.0 / axis_dim)
    lane = jnp.arange(LANES)
    use_cols = (lane % (2 * axis_dim)) // axis_dim == 1
    within = lane % axis_dim
    first = within < axis_dim // 2
    f = freqs[within % (axis_dim // 2)]
    pos = jnp.where(use_cols[None, :], cols[:, None], rows[:, None])
    ang = pos * f[None, :]
    cos, sin = jnp.cos(ang), jnp.sin(ang)
    sin_up = jnp.where(first[None, :], -sin, 0.0)
    sin_dn = jnp.where(first[None, :], 0.0, sin)
    pad = lambda a, fill: jnp.concatenate([jnp.full((n_ctx, LANES), fill, F32), a.astype(F32)], axis=0)
    return pad(cos, 1.0), pad(sin_up, 0.0), pad(sin_dn, 0.0)


def kernel(x, c, ctx, c_ctx, ada_w, ada_b, norm_pre, norm_post, ev_w_in, ev_w_out, ret_decay_logit, pool_w,
           pool_scale, od_w_in, od_w_out, diff_lambda, diff_subln):
    b, seq, d = x.shape
    n_ctx = ctx.shape[1]
    assert n_ctx == TM and seq % TM == 0 and (n_ctx + seq) % OUT_TM == 0 and d == D_MODEL

    xx = x

    mod_rows = ((b + 1 + 7) // 8) * 8
    c_aug = jnp.zeros((mod_rows, d), F32).at[:b].set(c).at[b].set(c_ctx)
    mod = _modulation(c_aug, ada_w, ada_b)

    lam_inits = tuple(0.8 - 0.6 * math.exp(-0.3 * l) for l in range(1, DEPTH, 2))
    lams = _diff_lambdas(diff_lambda, lam_inits)

    even_tabs = _rope_tables(n_ctx, seq, RET_DK // 2)
    odd_tabs = _rope_tables(n_ctx, seq, DIFF_DH // 2)

    for l in range(DEPTH):
        last = l == DEPTH - 1
        i = l // 2
        m = mod[l].reshape(mod_rows, 3, d)
        modsel = jnp.stack([jnp.broadcast_to(m[b], (b, 3, d)), m[:b]], axis=1)
        g_pre = norm_pre[l].reshape(1, d)
        g_post = norm_post[l].reshape(1, d)
        if l % 2 == 0:
            proj = _project(xx, modsel, g_pre, ev_w_in[i].astype(BF16), even_tabs, ctx=ctx if l == 0 else None,
                            n_rope=2 * RET_HEADS, n_q=RET_HEADS, rot=RET_DK // 4, k_scale=RET_DK ** -0.5)
            if l == 0:
                proj, xx = proj
            dl = jnp.broadcast_to(ret_decay_logit[i].T[:, :, None], (RET_HEADS, 2, LANES))
            z_ret = _retention(proj, dl, n_ctx)
            z_pool = _pooling(proj, pool_w[i].astype(BF16), pool_scale[i], ((0, n_ctx), (n_ctx, seq)))
            xx = _out_project(z_ret, z_pool, 0, ev_w_out[i].astype(BF16), xx, modsel, g_post, n_ctx=n_ctx,
                              latent_only=last)
        else:
            proj = _project(xx, modsel, g_pre, od_w_in[i].astype(BF16), odd_tabs,
                            n_rope=2 * DIFF_HEADS, n_q=2 * DIFF_HEADS, rot=DIFF_DH // 4, k_scale=1.0)
            z = _diff_attention(proj, lams[i], diff_subln[i], n_ctx=n_ctx, out_scale=1.0 - lam_inits[i],
                                skip_ctx=last)
            xx = _out_project(z, z, 1, od_w_out[i].astype(BF16), xx, modsel, g_post, n_ctx=n_ctx, latent_only=last)
    return xx
```

```python
import functools
import math

import jax
import jax.numpy as jnp
from jax import lax
from jax.experimental import pallas as pl
from jax.experimental.pallas import tpu as pltpu

F32 = jnp.float32
BF16 = jnp.bfloat16

D_MODEL = 1024
DEPTH = 4
GRID_W = 64
EPS = 1e-6
ROPE_BASE = 10000.0
LANES = 128

RET_HEADS = 4
RET_DK = 128
RET_CHUNK = 128
RET_HEADS_PER_STEP = 2
POOL_GROUPS = 4
POOL_WINDOWS = (2, 4, 8, 16)
POOL_PAD = 8
assert max(POOL_WINDOWS) // 2 <= POOL_PAD
EVEN_IN = 3072
EVEN_NB = EVEN_IN // LANES

DIFF_HEADS = 8
DIFF_DH = 64
ATTN_HEADS_PER_STEP = 8
ATTN_KEY_CHUNK = 768
LOG2_E = 1.4426950408889634
ODD_IN = 4096
ODD_NB = ODD_IN // LANES

TM = 256
OUT_TM = 768
LAST_OUT_TM = 1024
PROJ_WIDE_TM = 768
PROJ_TN = 512
VMEM_LIMIT = 48 * 1024 * 1024


def _cparams(n_axes):
    return pltpu.CompilerParams(dimension_semantics=("parallel",) * n_axes, vmem_limit_bytes=VMEM_LIMIT)


def _silu(x):
    return x * jax.nn.sigmoid(x)


def _mod_kernel(c_ref, w_ref, b_ref, o_ref):
    a = _silu(c_ref[...])
    o_ref[0] = jnp.dot(a, w_ref[0], preferred_element_type=F32, precision=lax.Precision.HIGHEST) + b_ref[0]


def _modulation(c_aug, ada_w, ada_b):
    rows = c_aug.shape[0]
    nt = 3 * D_MODEL // D_MODEL
    return pl.pallas_call(
        _mod_kernel,
        out_shape=jax.ShapeDtypeStruct((DEPTH, rows, 3 * D_MODEL), F32),
        grid=(DEPTH, nt),
        in_specs=[
            pl.BlockSpec((rows, D_MODEL), lambda l, n: (0, 0)),
            pl.BlockSpec((1, D_MODEL, D_MODEL), lambda l, n: (l, 0, n)),
            pl.BlockSpec((1, 1, D_MODEL), lambda l, n: (l, 0, n)),
        ],
        out_specs=pl.BlockSpec((1, rows, D_MODEL), lambda l, n: (l, 0, n)),
        compiler_params=_cparams(2),
        name="adaln_modulation",
    )(c_aug, ada_w, ada_b.reshape(DEPTH, 1, 3 * D_MODEL))


def _lam_kernel(lp_ref, o_ref, *, lam_inits):
    for i, lam_init in enumerate(lam_inits):
        lp = lp_ref[i]
        a = jnp.sum(lp[0:1] * lp[1:2], axis=-1, keepdims=True)
        b = jnp.sum(lp[2:3] * lp[3:4], axis=-1, keepdims=True)
        lam = jnp.exp(a) - jnp.exp(b) + lam_init
        o_ref[i] = jnp.broadcast_to(lam, (1, LANES))


def _diff_lambdas(diff_lambda, lam_inits):
    n = diff_lambda.shape[0]
    return pl.pallas_call(
        functools.partial(_lam_kernel, lam_inits=lam_inits),
        out_shape=jax.ShapeDtypeStruct((n, 1, LANES), F32),
        name="diff_lambda",
    )(diff_lambda)


def _proj_first_kernel(ctx_ref, *refs, n_sub, sub_rows, **static):
    x_refs, (*rest, o_ref, xx_ref) = refs[:n_sub], refs[n_sub:]
    at_start = pl.program_id(1) == 0

    def source(s):
        x = x_refs[s][0]
        if s == 0:
            x = jnp.where(at_start, ctx_ref[0], x)
        xx_ref[0, s * sub_rows:(s + 1) * sub_rows] = x
        return x

    _proj_sub_blocks(source, at_start, *rest, o_ref, n_sub=n_sub, sub_rows=sub_rows, **static)


def _proj_wide_kernel(x_ref, *rest, n_sub, sub_rows, **static):
    _proj_sub_blocks(lambda s: x_ref[0, s * sub_rows:(s + 1) * sub_rows], pl.program_id(1) == 0, *rest,
                     n_sub=n_sub, sub_rows=sub_rows, **static)


def _proj_sub_blocks(source, at_start, mod_ref, g_ref, w_ref, c_ref, s1_ref, s2_ref, o_ref, *, n_sub, sub_rows,
                     **static):
    def modulated(s):
        m = mod_ref[0, 1]
        if s == 0:
            m = jnp.where(at_start, mod_ref[0, 0], m)
        return _modulated_norm(source(s), m, g_ref)

    h = modulated(0)
    for s in range(n_sub):
        nxt = modulated(s + 1) if s + 1 < n_sub else None
        _proj_rows(h, slice(s * sub_rows, (s + 1) * sub_rows), w_ref, c_ref, s1_ref, s2_ref, o_ref, **static)
        h = nxt


def _modulated_norm(x, m, g_ref):
    y = x * lax.rsqrt(jnp.mean(x * x, axis=-1, keepdims=True) + EPS) * g_ref[...]
    return (y * (1.0 + m[1:2]) + m[0:1]).astype(BF16)


def _proj_rows(h, rows, w_ref, c_ref, s1_ref, s2_ref, o_ref, *, n_rope, n_q, rot, k_scale):
    cos, sin_up, sin_dn = c_ref[rows], s1_ref[rows], s2_ref[rows]
    per = PROJ_TN // LANES
    for n in range(w_ref.shape[1] // PROJ_TN):
        r = jnp.dot(h, w_ref[:, n * PROJ_TN:(n + 1) * PROJ_TN], preferred_element_type=F32)
        for cb in range(per):
            blk = n * per + cb
            v = r[:, cb * LANES:(cb + 1) * LANES]
            if blk < n_rope:
                v = v * cos + pltpu.roll(v, LANES - rot, 1) * sin_up + pltpu.roll(v, rot, 1) * sin_dn
                if blk >= n_q:
                    v = v * k_scale
            o_ref[0, blk, rows, :] = v.astype(BF16)


def _project(xx, modsel, g_pre, w_bf16, tables, *, ctx=None, **static):
    b, l, d = xx.shape
    first = ctx is not None
    if first:
        l += ctx.shape[1]
    n_in = w_bf16.shape[1]
    nb = n_in // LANES
    tm, sub = PROJ_WIDE_TM, TM
    n_sub = tm // sub
    tab_spec = pl.BlockSpec((tm, LANES), lambda i, j: (j, 0))
    common_specs = [
        pl.BlockSpec((1, 2, 3, d), lambda i, j: (i, 0, 0, 0)),
        pl.BlockSpec((1, d), lambda i, j: (0, 0)),
        pl.BlockSpec((d, n_in), lambda i, j: (0, 0)),
        tab_spec, tab_spec, tab_spec,
    ]
    proj_shape = jax.ShapeDtypeStruct((b, nb, l, LANES), BF16)
    proj_spec = pl.BlockSpec((1, nb, tm, LANES), lambda i, j: (i, 0, j, 0))
    if first:
        kern = functools.partial(_proj_first_kernel, n_sub=n_sub, sub_rows=sub, **static)
        src = (ctx,) + (xx,) * n_sub
        src_specs = [pl.BlockSpec((1, sub, d), lambda i, j: (i, 0, 0))] + [
            pl.BlockSpec((1, sub, d), lambda i, j, s=s: (i, jnp.maximum(n_sub * j + s - 1, 0), 0))
            for s in range(n_sub)]
        out_shape = (proj_shape, jax.ShapeDtypeStruct((b, l, d), F32))
        out_specs = (proj_spec, pl.BlockSpec((1, tm, d), lambda i, j: (i, j, 0)))
    else:
        kern = functools.partial(_proj_wide_kernel, n_sub=n_sub, sub_rows=sub, **static)
        src = (xx,)
        src_specs = [pl.BlockSpec((1, tm, d), lambda i, j: (i, j, 0))]
        out_shape, out_specs = proj_shape, proj_spec
    return pl.pallas_call(
        kern,
        out_shape=out_shape,
        grid=(b, l // tm),
        in_specs=src_specs + common_specs,
        out_specs=out_specs,
        compiler_params=_cparams(2),
        name="input_projection",
    )(*src, modsel, g_pre, w_bf16, *tables)


def _out_kernel(za_ref, zb_ref, w_ref, x_ref, mod_ref, g_ref, o_ref, *, first_row, n_ctx):
    half = za_ref.shape[-1]
    tm = x_ref.shape[1]
    y = jnp.dot(za_ref[0], w_ref[0:half], preferred_element_type=F32)
    y = y + jnp.dot(zb_ref[0], w_ref[half:2 * half], preferred_element_type=F32)
    r = y * lax.rsqrt(jnp.mean(y * y, axis=-1, keepdims=True) + EPS) * g_ref[...]
    row = first_row + pl.program_id(1) * tm + lax.broadcasted_iota(jnp.int32, (tm, 1), 0)
    gate = jnp.where(row < n_ctx, mod_ref[0, 0][2:3], mod_ref[0, 1][2:3])
    o_ref[0] = x_ref[0] + gate * r


def _out_project(za, zb, zb_col, w_bf16, xx, modsel, g_post, *, n_ctx, latent_only):
    b, l, d = xx.shape
    half = d // 2
    if latent_only:
        tm, first_row = LAST_OUT_TM, n_ctx
        z_first = first_row - (l - za.shape[1])
        el = pl.Element
        z_specs = [pl.BlockSpec((el(1), el(tm), el(half)),
                                lambda i, j, c=c: (i, pl.multiple_of(z_first + j * tm, n_ctx), c * half))
                   for c in (0, zb_col)]
        x_spec = pl.BlockSpec((el(1), el(tm), el(d)),
                              lambda i, j: (i, pl.multiple_of(first_row + j * tm, n_ctx), 0))
    else:
        tm, first_row = OUT_TM, 0
        z_specs = [pl.BlockSpec((1, tm, half), lambda i, j, c=c: (i, j, c)) for c in (0, zb_col)]
        x_spec = pl.BlockSpec((1, tm, d), lambda i, j: (i, j, 0))
    out_rows = l - first_row
    return pl.pallas_call(
        functools.partial(_out_kernel, first_row=first_row, n_ctx=n_ctx),
        out_shape=jax.ShapeDtypeStruct((b, out_rows, d), F32),
        grid=(b, out_rows // tm),
        in_specs=[
            *z_specs,
            pl.BlockSpec((d, d), lambda i, j: (0, 0)),
            x_spec,
            pl.BlockSpec((1, 2, 3, d), lambda i, j: (i, 0, 0, 0)),
            pl.BlockSpec((1, d), lambda i, j: (0, 0)),
        ],
        out_specs=pl.BlockSpec((1, tm, d), lambda i, j: (i, j, 0)),
        input_output_aliases={} if latent_only else {3: 0},
        compiler_params=_cparams(2),
        name="output_projection",
    )(za, zb, w_bf16, xx, modsel, g_post)


def _ret_kernel(dl_ref, q_ref, k_ref, v_ref, g_ref, o_ref, *, n_ctx_chunks):
    o_ref[0] = jnp.concatenate(
        [_ret_head(dl_ref, q_ref, k_ref, v_ref, g_ref, hh, n_ctx_chunks) for hh in range(q_ref.shape[1])], axis=1)


def _ret_head(dl_ref, q_ref, k_ref, v_ref, g_ref, hh, n_ctx_chunks):
    c = RET_CHUNK
    n_chunks = q_ref.shape[2] // c
    lg = -jnp.exp(dl_ref[hh])
    lgf, lgb = lg[0:1], lg[1:2]
    ri = lax.broadcasted_iota(jnp.int32, (c, c), 0).astype(F32)
    ci = lax.broadcasted_iota(jnp.int32, (c, c), 1).astype(F32)
    rel = ri - ci
    decay = jnp.where(rel >= 0.0, jnp.exp(lgf * jnp.maximum(rel, 0.0)), jnp.exp(lgb * jnp.maximum(-rel, 0.0)))
    q_dec_f = jnp.exp(lgf * (ri + 1.0))
    q_dec_b = jnp.exp(lgb * (c - ri))
    k_dec_f = jnp.exp(lgf * (c - 1.0 - ri))
    k_dec_b = jnp.exp(lgb * ri)
    c_dec_f = jnp.exp(lgf * c)
    c_dec_b = jnp.exp(lgb * c)

    def chunk(ref, i):
        return ref[0, hh, i * c:(i + 1) * c, :]

    updates = []
    for i in range(n_chunks):
        kc = chunk(k_ref, i).astype(F32)
        kk = jnp.concatenate([kc * k_dec_f, kc * k_dec_b], axis=1).astype(BF16)
        updates.append(lax.dot_general(kk, chunk(v_ref, i), (((0,), (0,)), ((), ())), preferred_element_type=F32))

    before = [None] * n_chunks
    state = jnp.zeros((c, LANES), F32)
    for i in range(n_chunks):
        before[i] = [state]
        state = state * c_dec_f + updates[i][0:c]
    order = list(range(n_ctx_chunks - 1, -1, -1)) + list(range(n_chunks - 1, n_ctx_chunks - 1, -1))
    state = jnp.zeros((c, LANES), F32)
    for i in order:
        before[i].append(state)
        state = state * c_dec_b + updates[i][c:2 * c]

    masked = [(lax.dot_general(chunk(q_ref, i), chunk(k_ref, i), (((1,), (1,)), ((), ())),
                               preferred_element_type=F32) * decay).astype(BF16) for i in range(n_chunks)]
    crosses = []
    for i in range(n_chunks):
        qf = chunk(q_ref, i).astype(F32)
        qq = jnp.concatenate([qf * q_dec_f, qf * q_dec_b], axis=1).astype(BF16)
        states = jnp.concatenate(before[i], axis=0).astype(BF16)
        crosses.append(jnp.dot(qq, states, preferred_element_type=F32))
    o = jnp.concatenate([jnp.dot(masked[i], chunk(v_ref, i), preferred_element_type=F32) + crosses[i]
                         for i in range(n_chunks)], axis=0)
    mu = jnp.mean(o, axis=-1, keepdims=True)
    dev = o - mu
    var = jnp.mean(dev * dev, axis=-1, keepdims=True)
    g = g_ref[0, hh].astype(F32)
    return (dev * lax.rsqrt(var + EPS) * _silu(g)).astype(BF16)


def _retention(proj, decay_logit_lanes, n_ctx):
    b, _, l, _ = proj.shape
    hp = RET_HEADS_PER_STEP
    ng = RET_HEADS // hp

    def col(kind):
        return pl.BlockSpec((1, hp, l, LANES), lambda i, h: (i, kind * ng + h, 0, 0))

    return pl.pallas_call(
        functools.partial(_ret_kernel, n_ctx_chunks=n_ctx // RET_CHUNK),
        out_shape=jax.ShapeDtypeStruct((b, l, RET_HEADS * LANES), BF16),
        grid=(b, ng),
        in_specs=[
            pl.BlockSpec((hp, 2, LANES), lambda i, h: (h, 0, 0)),
            col(0), col(1), col(2), col(3),
        ],
        out_specs=pl.BlockSpec((1, l, hp * LANES), lambda i, h: (i, 0, h)),
        compiler_params=_cparams(2),
        name="retention",
    )(decay_logit_lanes, proj, proj, proj, proj)


def _pool_kernel(p_ref, g_ref, w_ref, sc_ref, o_ref, pad_scr, *, segments):
    for grp, win in enumerate(POOL_WINDOWS):
        below = win // 2
        above = win - below - 1
        base = 0
        for start, length in segments:
            zeros = jnp.zeros((POOL_PAD, LANES), F32)
            pad_scr[base:base + POOL_PAD] = zeros
            pad_scr[base + POOL_PAD:base + POOL_PAD + length] = p_ref[0, grp, start:start + length, :].astype(F32)
            pad_scr[base + POOL_PAD + length:base + 2 * POOL_PAD + length] = zeros
            base += length + 2 * POOL_PAD
        base = 0
        for start, length in segments:
            first = base + POOL_PAD
            total = pad_scr[first - below:first - below + length]
            for off in range(-below + 1, above + 1):
                total = total + pad_scr[first + off:first + off + length]
            edge = lax.broadcasted_iota(jnp.int32, (POOL_PAD, LANES), 0)

            def edge_mean(first_pos):
                pos = first_pos + edge
                count = jnp.minimum(pos + above, length - 1) - jnp.maximum(pos - below, 0) + 1
                return total[first_pos:first_pos + POOL_PAD] / count.astype(F32)

            mean = jnp.concatenate([edge_mean(0), total[POOL_PAD:length - POOL_PAD] * (1.0 / win),
                                    edge_mean(length - POOL_PAD)], axis=0)
            d = mean - pad_scr[first:first + length]
            y = jnp.dot(d.astype(BF16), w_ref[grp], preferred_element_type=F32) * sc_ref[grp]
            g = g_ref[0, grp, start:start + length, :].astype(F32)
            o_ref[0, start:start + length, grp * LANES:(grp + 1) * LANES] = (y * _silu(g)).astype(BF16)
            base += length + 2 * POOL_PAD


def _pooling(proj, pool_w_bf16, pool_scale, segments):
    b, _, l, _ = proj.shape
    g = POOL_GROUPS
    pad_rows = l + 2 * POOL_PAD * len(segments)
    return pl.pallas_call(
        functools.partial(_pool_kernel, segments=segments),
        out_shape=jax.ShapeDtypeStruct((b, l, g * LANES), BF16),
        grid=(b,),
        in_specs=[
            pl.BlockSpec((1, g, l, LANES), lambda i: (i, 4, 0, 0)),
            pl.BlockSpec((1, g, l, LANES), lambda i: (i, 5, 0, 0)),
            pl.BlockSpec((g, LANES, LANES), lambda i: (0, 0, 0)),
            pl.BlockSpec((g, 1, LANES), lambda i: (0, 0, 0)),
        ],
        out_specs=pl.BlockSpec((1, l, g * LANES), lambda i: (i, 0, 0)),
        scratch_shapes=[pltpu.VMEM((pad_rows, LANES), F32)],
        compiler_params=_cparams(1),
        name="pool_mix",
    )(proj, proj, pool_w_bf16, pool_scale.reshape(g, 1, LANES))


def _attn_kernel(lam_ref, sub_ref, q_ref, k_ref, v_ref, g_ref, o_ref, *, n_ctx, out_scale, skip_ctx):
    lam = lam_ref[...][:, 0:1]
    lane = lax.broadcasted_iota(jnp.int32, (q_ref.shape[2], LANES), 1)

    def attend(n_keys):
        n_heads, tq = q_ref.shape[1], q_ref.shape[2]
        chunk = min(ATTN_KEY_CHUNK, n_keys)
        n_chunks = n_keys // chunk

        def stacked_q(hh):
            q = q_ref[0, hh] * (DIFF_DH ** -0.5 * LOG2_E)
            zero = jnp.zeros_like(q)
            return jnp.concatenate([jnp.where(lane < DIFF_DH, q, zero), jnp.where(lane >= DIFF_DH, q, zero)], axis=0)

        def keys(t):
            return slice(t * chunk, (t + 1) * chunk)

        outs = []
        scored, exped = None, None
        for i in range(n_heads + 2):
            qs = stacked_q(i) if i < n_heads else None
            s_chunks, m_acc, p_chunks, l_acc, o_t = [], None, [], None, None
            if exped is not None:
                l0, l1 = exped[1][:, 0:tq], exped[1][:, tq:2 * tq]
                ratio = lam * l0 / l1
            for t in range(n_chunks):
                if qs is not None:
                    s_t = lax.dot_general(k_ref[0, i, keys(t), :], qs, (((1,), (1,)), ((), ())),
                                          preferred_element_type=F32)
                    s_chunks.append(s_t)
                    m_t = jnp.max(s_t, axis=0, keepdims=True)
                    m_acc = m_t if m_acc is None else jnp.maximum(m_acc, m_t)
                if scored is not None:
                    p_t = jnp.exp2(scored[0][t] - scored[1])
                    p_chunks.append(p_t)
                    l_t = jnp.sum(p_t, axis=0, keepdims=True)
                    l_acc = l_t if l_acc is None else l_acc + l_t
                if exped is not None:
                    a_t = (exped[0][t][:, 0:tq] - exped[0][t][:, tq:2 * tq] * ratio).astype(BF16)
                    part = lax.dot_general(v_ref[0, i - 2, keys(t), :], a_t, (((0,), (0,)), ((), ())),
                                           preferred_element_type=F32)
                    o_t = part if o_t is None else o_t + part
            if exped is not None:
                o_t = o_t * (1.0 / l0)
                o_t = o_t * lax.rsqrt(jnp.mean(o_t * o_t, axis=0, keepdims=True) + EPS)
                o = o_t.T * sub_ref[...] * out_scale
                g = g_ref[0, i - 2].astype(F32)
                outs.append((o * _silu(g)).astype(BF16))
            exped = (p_chunks, l_acc) if scored is not None else None
            scored = (s_chunks, m_acc) if qs is not None else None
        o_ref[0] = jnp.concatenate(outs, axis=1)

    if skip_ctx:
        attend(k_ref.shape[2])
        return
    j = pl.program_id(2)

    @pl.when(j == 0)
    def _():
        attend(n_ctx)

    @pl.when(j > 0)
    def _():
        attend(k_ref.shape[2])


def _diff_attention(proj, lam, subln, *, n_ctx, out_scale, skip_ctx):
    b, _, l, _ = proj.shape
    hp = ATTN_HEADS_PER_STEP
    ng = DIFF_HEADS // hp
    skip = 1 if skip_ctx else 0
    kern = functools.partial(_attn_kernel, n_ctx=n_ctx, out_scale=out_scale, skip_ctx=skip_ctx)
    return pl.pallas_call(
        kern,
        out_shape=jax.ShapeDtypeStruct((b, l - skip * TM, DIFF_HEADS * LANES), BF16),
        grid=(b, ng, l // TM - skip),
        in_specs=[
            pl.BlockSpec((1, LANES), lambda i, hh, j: (0, 0)),
            pl.BlockSpec((1, LANES), lambda i, hh, j: (0, 0)),
            pl.BlockSpec((1, hp, TM, LANES), lambda i, hh, j: (i, hh, j + skip, 0)),
            pl.BlockSpec((1, hp, l, LANES), lambda i, hh, j: (i, ng + hh, 0, 0)),
            pl.BlockSpec((1, hp, l, LANES), lambda i, hh, j: (i, 2 * ng + hh, 0, 0)),
            pl.BlockSpec((1, hp, TM, LANES), lambda i, hh, j: (i, 3 * ng + hh, j + skip, 0)),
        ],
        out_specs=pl.BlockSpec((1, TM, hp * LANES), lambda i, hh, j: (i, j, hh)),
        compiler_params=_cparams(3),
        name="diff_attention",
    )(lam, subln.reshape(1, LANES), proj, proj, proj, proj)


def _rope_tables(n_ctx, seq, axis_dim):
    t = jnp.arange(seq)
    rows = (t // GRID_W).astype(F32)
    cols = (t % GRID_W).astype(F32)
    freqs = ROPE_BASE ** (-jnp.arange(axis_dim // 2, dtype=F32) * 2.0 / axis_dim)
    lane = jnp.arange(LANES)
    use_cols = (lane % (2 * axis_dim)) // axis_dim == 1
    within = lane % axis_dim
    first = within < axis_dim // 2
    f = freqs[within % (axis_dim // 2)]
    pos = jnp.where(use_cols[None, :], cols[:, None], rows[:, None])
    ang = pos * f[None, :]
    cos, sin = jnp.cos(ang), jnp.sin(ang)
    sin_up = jnp.where(first[None, :], -sin, 0.0)
    sin_dn = jnp.where(first[None, :], 0.0, sin)
    pad = lambda a, fill: jnp.concatenate([jnp.full((n_ctx, LANES), fill, F32), a.astype(F32)], axis=0)
    return pad(cos, 1.0), pad(sin_up, 0.0), pad(sin_dn, 0.0)


def kernel(x, c, ctx, c_ctx, ada_w, ada_b, norm_pre, norm_post, ev_w_in, ev_w_out, ret_decay_logit, pool_w,
           pool_scale, od_w_in, od_w_out, diff_lambda, diff_subln):
    b, seq, d = x.shape
    n_ctx = ctx.shape[1]
    assert n_ctx == TM and seq % TM == 0 and (n_ctx + seq) % OUT_TM == 0 and d == D_MODEL

    xx = x

    mod_rows = ((b + 1 + 7) // 8) * 8
    c_aug = jnp.zeros((mod_rows, d), F32).at[:b].set(c).at[b].set(c_ctx)
    mod = _modulation(c_aug, ada_w, ada_b)

    lam_inits = tuple(0.8 - 0.6 * math.exp(-0.3 * l) for l in range(1, DEPTH, 2))
    lams = _diff_lambdas(diff_lambda, lam_inits)

    even_tabs = _rope_tables(n_ctx, seq, RET_DK // 2)
    odd_tabs = _rope_tables(n_ctx, seq, DIFF_DH // 2)

    for l in range(DEPTH):
        last = l == DEPTH - 1
        i = l // 2
        m = mod[l].reshape(mod_rows, 3, d)
        modsel = jnp.stack([jnp.broadcast_to(m[b], (b, 3, d)), m[:b]], axis=1)
        g_pre = norm_pre[l].reshape(1, d)
        g_post = norm_post[l].reshape(1, d)
        if l % 2 == 0:
            proj = _project(xx, modsel, g_pre, ev_w_in[i].astype(BF16), even_tabs, ctx=ctx if l == 0 else None,
                            n_rope=2 * RET_HEADS, n_q=RET_HEADS, rot=RET_DK // 4, k_scale=RET_DK ** -0.5)
            if l == 0:
                proj, xx = proj
            dl = jnp.broadcast_to(ret_decay_logit[i].T[:, :, None], (RET_HEADS, 2, LANES))
            z_ret = _retention(proj, dl, n_ctx)
            z_pool = _pooling(proj, pool_w[i].astype(BF16), pool_scale[i], ((0, n_ctx), (n_ctx, seq)))
            xx = _out_project(z_ret, z_pool, 0, ev_w_out[i].astype(BF16), xx, modsel, g_post, n_ctx=n_ctx,
                              latent_only=last)
        else:
            proj = _project(xx, modsel, g_pre, od_w_in[i].astype(BF16), odd_tabs,
                            n_rope=2 * DIFF_HEADS, n_q=2 * DIFF_HEADS, rot=DIFF_DH // 4, k_scale=1.0)
            z = _diff_attention(proj, lams[i], diff_subln[i], n_ctx=n_ctx, out_scale=1.0 - lam_inits[i],
                                skip_ctx=last)
            xx = _out_project(z, z, 1, od_w_out[i].astype(BF16), xx, modsel, g_post, n_ctx=n_ctx, latent_only=last)
    return xx
```

```python
import functools
import math

import jax
import jax.numpy as jnp
from jax import lax
from jax.experimental import pallas as pl
from jax.experimental.pallas import tpu as pltpu

F32 = jnp.float32
BF16 = jnp.bfloat16

D_MODEL = 1024
DEPTH = 4
GRID_W = 64
EPS = 1e-6
ROPE_BASE = 10000.0
LANES = 128

RET_HEADS = 4
RET_DK = 128
RET_CHUNK = 128
RET_HEADS_PER_STEP = 4
POOL_GROUPS = 4
POOL_WINDOWS = (2, 4, 8, 16)
POOL_PAD = 8
assert max(POOL_WINDOWS) // 2 <= POOL_PAD
EVEN_IN = 3072
EVEN_NB = EVEN_IN // LANES

DIFF_HEADS = 8
DIFF_DH = 64
ATTN_HEADS_PER_STEP = 8
ATTN_KEY_CHUNK = 768
LOG2_E = 1.4426950408889634
ODD_IN = 4096
ODD_NB = ODD_IN // LANES

TM = 256
OUT_TM = 768
LAST_OUT_TM = 1024
PROJ_WIDE_TM = 768
PROJ_TN = 512
VMEM_LIMIT = 48 * 1024 * 1024


def _cparams(n_axes):
    return pltpu.CompilerParams(dimension_semantics=("parallel",) * n_axes, vmem_limit_bytes=VMEM_LIMIT)


def _silu(x):
    return x * jax.nn.sigmoid(x)


def _mod_kernel(c_ref, w_ref, b_ref, o_ref):
    a = _silu(c_ref[...])
    o_ref[0] = jnp.dot(a, w_ref[0], preferred_element_type=F32, precision=lax.Precision.HIGHEST) + b_ref[0]


def _modulation(c_aug, ada_w, ada_b):
    rows = c_aug.shape[0]
    nt = 3 * D_MODEL // D_MODEL
    return pl.pallas_call(
        _mod_kernel,
        out_shape=jax.ShapeDtypeStruct((DEPTH, rows, 3 * D_MODEL), F32),
        grid=(DEPTH, nt),
        in_specs=[
            pl.BlockSpec((rows, D_MODEL), lambda l, n: (0, 0)),
            pl.BlockSpec((1, D_MODEL, D_MODEL), lambda l, n: (l, 0, n)),
            pl.BlockSpec((1, 1, D_MODEL), lambda l, n: (l, 0, n)),
        ],
        out_specs=pl.BlockSpec((1, rows, D_MODEL), lambda l, n: (l, 0, n)),
        compiler_params=_cparams(2),
        name="adaln_modulation",
    )(c_aug, ada_w, ada_b.reshape(DEPTH, 1, 3 * D_MODEL))


def _lam_kernel(lp_ref, o_ref, *, lam_inits):
    for i, lam_init in enumerate(lam_inits):
        lp = lp_ref[i]
        a = jnp.sum(lp[0:1] * lp[1:2], axis=-1, keepdims=True)
        b = jnp.sum(lp[2:3] * lp[3:4], axis=-1, keepdims=True)
        lam = jnp.exp(a) - jnp.exp(b) + lam_init
        o_ref[i] = jnp.broadcast_to(lam, (1, LANES))


def _diff_lambdas(diff_lambda, lam_inits):
    n = diff_lambda.shape[0]
    return pl.pallas_call(
        functools.partial(_lam_kernel, lam_inits=lam_inits),
        out_shape=jax.ShapeDtypeStruct((n, 1, LANES), F32),
        name="diff_lambda",
    )(diff_lambda)


def _proj_first_kernel(ctx_ref, *refs, n_sub, sub_rows, **static):
    x_refs, (*rest, o_ref, xx_ref) = refs[:n_sub], refs[n_sub:]
    at_start = pl.program_id(1) == 0

    def source(s):
        x = x_refs[s][0]
        if s == 0:
            x = jnp.where(at_start, ctx_ref[0], x)
        xx_ref[0, s * sub_rows:(s + 1) * sub_rows] = x
        return x

    _proj_sub_blocks(source, at_start, *rest, o_ref, n_sub=n_sub, sub_rows=sub_rows, **static)


def _proj_wide_kernel(x_ref, *rest, n_sub, sub_rows, **static):
    _proj_sub_blocks(lambda s: x_ref[0, s * sub_rows:(s + 1) * sub_rows], pl.program_id(1) == 0, *rest,
                     n_sub=n_sub, sub_rows=sub_rows, **static)


def _proj_sub_blocks(source, at_start, mod_ref, g_ref, w_ref, c_ref, s1_ref, s2_ref, o_ref, *, n_sub, sub_rows,
                     **static):
    def modulated(s):
        m = mod_ref[0, 1]
        if s == 0:
            m = jnp.where(at_start, mod_ref[0, 0], m)
        return _modulated_norm(source(s), m, g_ref)

    h = modulated(0)
    for s in range(n_sub):
        nxt = modulated(s + 1) if s + 1 < n_sub else None
        _proj_rows(h, slice(s * sub_rows, (s + 1) * sub_rows), w_ref, c_ref, s1_ref, s2_ref, o_ref, **static)
        h = nxt


def _modulated_norm(x, m, g_ref):
    y = x * lax.rsqrt(jnp.mean(x * x, axis=-1, keepdims=True) + EPS) * g_ref[...]
    return (y * (1.0 + m[1:2]) + m[0:1]).astype(BF16)


def _proj_rows(h, rows, w_ref, c_ref, s1_ref, s2_ref, o_ref, *, n_rope, n_q, rot, k_scale):
    cos, sin_up, sin_dn = c_ref[rows], s1_ref[rows], s2_ref[rows]
    per = PROJ_TN // LANES
    for n in range(w_ref.shape[1] // PROJ_TN):
        r = jnp.dot(h, w_ref[:, n * PROJ_TN:(n + 1) * PROJ_TN], preferred_element_type=F32)
        for cb in range(per):
            blk = n * per + cb
            v = r[:, cb * LANES:(cb + 1) * LANES]
            if blk < n_rope:
                v = v * cos + pltpu.roll(v, LANES - rot, 1) * sin_up + pltpu.roll(v, rot, 1) * sin_dn
                if blk >= n_q:
                    v = v * k_scale
            o_ref[0, blk, rows, :] = v.astype(BF16)


def _project(xx, modsel, g_pre, w_bf16, tables, *, ctx=None, **static):
    b, l, d = xx.shape
    first = ctx is not None
    if first:
        l += ctx.shape[1]
    n_in = w_bf16.shape[1]
    nb = n_in // LANES
    tm, sub = PROJ_WIDE_TM, TM
    n_sub = tm // sub
    tab_spec = pl.BlockSpec((tm, LANES), lambda i, j: (j, 0))
    common_specs = [
        pl.BlockSpec((1, 2, 3, d), lambda i, j: (i, 0, 0, 0)),
        pl.BlockSpec((1, d), lambda i, j: (0, 0)),
        pl.BlockSpec((d, n_in), lambda i, j: (0, 0)),
        tab_spec, tab_spec, tab_spec,
    ]
    proj_shape = jax.ShapeDtypeStruct((b, nb, l, LANES), BF16)
    proj_spec = pl.BlockSpec((1, nb, tm, LANES), lambda i, j: (i, 0, j, 0))
    if first:
        kern = functools.partial(_proj_first_kernel, n_sub=n_sub, sub_rows=sub, **static)
        src = (ctx,) + (xx,) * n_sub
        src_specs = [pl.BlockSpec((1, sub, d), lambda i, j: (i, 0, 0))] + [
            pl.BlockSpec((1, sub, d), lambda i, j, s=s: (i, jnp.maximum(n_sub * j + s - 1, 0), 0))
            for s in range(n_sub)]
        out_shape = (proj_shape, jax.ShapeDtypeStruct((b, l, d), F32))
        out_specs = (proj_spec, pl.BlockSpec((1, tm, d), lambda i, j: (i, j, 0)))
    else:
        kern = functools.partial(_proj_wide_kernel, n_sub=n_sub, sub_rows=sub, **static)
        src = (xx,)
        src_specs = [pl.BlockSpec((1, tm, d), lambda i, j: (i, j, 0))]
        out_shape, out_specs = proj_shape, proj_spec
    return pl.pallas_call(
        kern,
        out_shape=out_shape,
        grid=(b, l // tm),
        in_specs=src_specs + common_specs,
        out_specs=out_specs,
        compiler_params=_cparams(2),
        name="input_projection",
    )(*src, modsel, g_pre, w_bf16, *tables)


def _out_kernel(za_ref, zb_ref, w_ref, x_ref, mod_ref, g_ref, o_ref, *, first_row, n_ctx):
    half = za_ref.shape[-1]
    tm = x_ref.shape[1]
    sub = n_ctx
    step_row = first_row + pl.program_id(1) * tm

    def project(s):
        rows = slice(s * sub, (s + 1) * sub)
        y = jnp.dot(za_ref[0, rows], w_ref[0:half], preferred_element_type=F32)
        return y + jnp.dot(zb_ref[0, rows], w_ref[half:2 * half], preferred_element_type=F32)

    def finish(s, y):
        rows = slice(s * sub, (s + 1) * sub)
        r = y * lax.rsqrt(jnp.mean(y * y, axis=-1, keepdims=True) + EPS) * g_ref[...]
        row = step_row + s * sub + lax.broadcasted_iota(jnp.int32, (sub, 1), 0)
        gate = jnp.where(row < n_ctx, mod_ref[0, 0][2:3], mod_ref[0, 1][2:3])
        o_ref[0, rows] = x_ref[0, rows] + gate * r

    y = project(0)
    for s in range(tm // sub):
        nxt = project(s + 1) if (s + 1) * sub < tm else None
        finish(s, y)
        y = nxt


def _out_project(za, zb, zb_col, w_bf16, xx, modsel, g_post, *, n_ctx, latent_only):
    b, l, d = xx.shape
    half = d // 2
    if latent_only:
        tm, first_row = LAST_OUT_TM, n_ctx
        z_first = first_row - (l - za.shape[1])
        el = pl.Element
        z_specs = [pl.BlockSpec((el(1), el(tm), el(half)),
                                lambda i, j, c=c: (i, pl.multiple_of(z_first + j * tm, n_ctx), c * half))
                   for c in (0, zb_col)]
        x_spec = pl.BlockSpec((el(1), el(tm), el(d)),
                              lambda i, j: (i, pl.multiple_of(first_row + j * tm, n_ctx), 0))
    else:
        tm, first_row = OUT_TM, 0
        z_specs = [pl.BlockSpec((1, tm, half), lambda i, j, c=c: (i, j, c)) for c in (0, zb_col)]
        x_spec = pl.BlockSpec((1, tm, d), lambda i, j: (i, j, 0))
    out_rows = l - first_row
    return pl.pallas_call(
        functools.partial(_out_kernel, first_row=first_row, n_ctx=n_ctx),
        out_shape=jax.ShapeDtypeStruct((b, out_rows, d), F32),
        grid=(b, out_rows // tm),
        in_specs=[
            *z_specs,
            pl.BlockSpec((d, d), lambda i, j: (0, 0)),
            x_spec,
            pl.BlockSpec((1, 2, 3, d), lambda i, j: (i, 0, 0, 0)),
            pl.BlockSpec((1, d), lambda i, j: (0, 0)),
        ],
        out_specs=pl.BlockSpec((1, tm, d), lambda i, j: (i, j, 0)),
        input_output_aliases={} if latent_only else {3: 0},
        compiler_params=_cparams(2),
        name="output_projection",
    )(za, zb, w_bf16, xx, modsel, g_post)


def _ret_kernel(dl_ref, q_ref, k_ref, v_ref, g_ref, o_ref, *, n_ctx_chunks):
    o_ref[0] = jnp.concatenate(
        [_ret_head(dl_ref, q_ref, k_ref, v_ref, g_ref, hh, n_ctx_chunks) for hh in range(q_ref.shape[1])], axis=1)


def _ret_head(dl_ref, q_ref, k_ref, v_ref, g_ref, hh, n_ctx_chunks):
    c = RET_CHUNK
    n_chunks = q_ref.shape[2] // c
    lg = -jnp.exp(dl_ref[hh])
    lgf, lgb = lg[0:1], lg[1:2]
    ri = lax.broadcasted_iota(jnp.int32, (c, c), 0).astype(F32)
    ci = lax.broadcasted_iota(jnp.int32, (c, c), 1).astype(F32)
    rel = ri - ci
    decay = jnp.where(rel >= 0.0, jnp.exp(lgf * jnp.maximum(rel, 0.0)), jnp.exp(lgb * jnp.maximum(-rel, 0.0)))
    q_dec_f = jnp.exp(lgf * (ri + 1.0))
    q_dec_b = jnp.exp(lgb * (c - ri))
    k_dec_f = jnp.exp(lgf * (c - 1.0 - ri))
    k_dec_b = jnp.exp(lgb * ri)
    c_dec_f = jnp.exp(lgf * c)
    c_dec_b = jnp.exp(lgb * c)

    def chunk(ref, i):
        return ref[0, hh, i * c:(i + 1) * c, :]

    updates = []
    for i in range(n_chunks):
        kc = chunk(k_ref, i).astype(F32)
        kk = jnp.concatenate([kc * k_dec_f, kc * k_dec_b], axis=1).astype(BF16)
        updates.append(lax.dot_general(kk, chunk(v_ref, i), (((0,), (0,)), ((), ())), preferred_element_type=F32))

    before = [None] * n_chunks
    state = jnp.zeros((c, LANES), F32)
    for i in range(n_chunks):
        before[i] = [state]
        state = state * c_dec_f + updates[i][0:c]
    order = list(range(n_ctx_chunks - 1, -1, -1)) + list(range(n_chunks - 1, n_ctx_chunks - 1, -1))
    state = jnp.zeros((c, LANES), F32)
    for i in order:
        before[i].append(state)
        state = state * c_dec_b + updates[i][c:2 * c]

    masked = [(lax.dot_general(chunk(q_ref, i), chunk(k_ref, i), (((1,), (1,)), ((), ())),
                               preferred_element_type=F32) * decay).astype(BF16) for i in range(n_chunks)]
    crosses = []
    for i in range(n_chunks):
        qf = chunk(q_ref, i).astype(F32)
        qq = jnp.concatenate([qf * q_dec_f, qf * q_dec_b], axis=1).astype(BF16)
        states = jnp.concatenate(before[i], axis=0).astype(BF16)
        crosses.append(jnp.dot(qq, states, preferred_element_type=F32))
    o = jnp.concatenate([jnp.dot(masked[i], chunk(v_ref, i), preferred_element_type=F32) + crosses[i]
                         for i in range(n_chunks)], axis=0)
    mu = jnp.mean(o, axis=-1, keepdims=True)
    dev = o - mu
    var = jnp.mean(dev * dev, axis=-1, keepdims=True)
    g = g_ref[0, hh].astype(F32)
    return (dev * lax.rsqrt(var + EPS) * _silu(g)).astype(BF16)


def _retention(proj, decay_logit_lanes, n_ctx):
    b, _, l, _ = proj.shape
    hp = RET_HEADS_PER_STEP
    ng = RET_HEADS // hp

    def col(kind):
        return pl.BlockSpec((1, hp, l, LANES), lambda i, h: (i, kind * ng + h, 0, 0))

    return pl.pallas_call(
        functools.partial(_ret_kernel, n_ctx_chunks=n_ctx // RET_CHUNK),
        out_shape=jax.ShapeDtypeStruct((b, l, RET_HEADS * LANES), BF16),
        grid=(b, ng),
        in_specs=[
            pl.BlockSpec((hp, 2, LANES), lambda i, h: (h, 0, 0)),
            col(0), col(1), col(2), col(3),
        ],
        out_specs=pl.BlockSpec((1, l, hp * LANES), lambda i, h: (i, 0, h)),
        compiler_params=_cparams(2),
        name="retention",
    )(decay_logit_lanes, proj, proj, proj, proj)


def _pool_kernel(p_ref, g_ref, w_ref, sc_ref, o_ref, pad_scr, *, segments):
    for grp, win in enumerate(POOL_WINDOWS):
        below = win // 2
        above = win - below - 1
        base = 0
        for start, length in segments:
            zeros = jnp.zeros((POOL_PAD, LANES), F32)
            pad_scr[base:base + POOL_PAD] = zeros
            pad_scr[base + POOL_PAD:base + POOL_PAD + length] = p_ref[0, grp, start:start + length, :].astype(F32)
            pad_scr[base + POOL_PAD + length:base + 2 * POOL_PAD + length] = zeros
            base += length + 2 * POOL_PAD
        base = 0
        for start, length in segments:
            first = base + POOL_PAD
            total = pad_scr[first - below:first - below + length]
            for off in range(-below + 1, above + 1):
                total = total + pad_scr[first + off:first + off + length]
            edge = lax.broadcasted_iota(jnp.int32, (POOL_PAD, LANES), 0)

            def edge_mean(first_pos):
                pos = first_pos + edge
                count = jnp.minimum(pos + above, length - 1) - jnp.maximum(pos - below, 0) + 1
                return total[first_pos:first_pos + POOL_PAD] / count.astype(F32)

            mean = jnp.concatenate([edge_mean(0), total[POOL_PAD:length - POOL_PAD] * (1.0 / win),
                                    edge_mean(length - POOL_PAD)], axis=0)
            d = mean - pad_scr[first:first + length]
            y = jnp.dot(d.astype(BF16), w_ref[grp], preferred_element_type=F32) * sc_ref[grp]
            g = g_ref[0, grp, start:start + length, :].astype(F32)
            o_ref[0, start:start + length, grp * LANES:(grp + 1) * LANES] = (y * _silu(g)).astype(BF16)
            base += length + 2 * POOL_PAD


def _pooling(proj, pool_w_bf16, pool_scale, segments):
    b, _, l, _ = proj.shape
    g = POOL_GROUPS
    pad_rows = l + 2 * POOL_PAD * len(segments)
    return pl.pallas_call(
        functools.partial(_pool_kernel, segments=segments),
        out_shape=jax.ShapeDtypeStruct((b, l, g * LANES), BF16),
        grid=(b,),
        in_specs=[
            pl.BlockSpec((1, g, l, LANES), lambda i: (i, 4, 0, 0)),
            pl.BlockSpec((1, g, l, LANES), lambda i: (i, 5, 0, 0)),
            pl.BlockSpec((g, LANES, LANES), lambda i: (0, 0, 0)),
            pl.BlockSpec((g, 1, LANES), lambda i: (0, 0, 0)),
        ],
        out_specs=pl.BlockSpec((1, l, g * LANES), lambda i: (i, 0, 0)),
        scratch_shapes=[pltpu.VMEM((pad_rows, LANES), F32)],
        compiler_params=_cparams(1),
        name="pool_mix",
    )(proj, proj, pool_w_bf16, pool_scale.reshape(g, 1, LANES))


def _attn_kernel(lam_ref, sub_ref, q_ref, k_ref, v_ref, g_ref, o_ref, *, n_ctx, out_scale, skip_ctx):
    lam = lam_ref[...][:, 0:1]
    lane = lax.broadcasted_iota(jnp.int32, (q_ref.shape[2], LANES), 1)

    def attend(n_keys):
        n_heads, tq = q_ref.shape[1], q_ref.shape[2]
        chunk = min(ATTN_KEY_CHUNK, n_keys)
        n_chunks = n_keys // chunk

        def stacked_q(hh):
            q = q_ref[0, hh] * (DIFF_DH ** -0.5 * LOG2_E)
            zero = jnp.zeros_like(q)
            return jnp.concatenate([jnp.where(lane < DIFF_DH, q, zero), jnp.where(lane >= DIFF_DH, q, zero)], axis=0)

        def keys(t):
            return slice(t * chunk, (t + 1) * chunk)

        outs = []
        scored, exped = None, None
        for i in range(n_heads + 2):
            qs = stacked_q(i) if i < n_heads else None
            s_chunks, m_acc, p_chunks, l_acc, o_t = [], None, [], None, None
            if exped is not None:
                l0, l1 = exped[1][:, 0:tq], exped[1][:, tq:2 * tq]
                ratio = lam * l0 / l1
            for t in range(n_chunks):
                if qs is not None:
                    s_t = lax.dot_general(k_ref[0, i, keys(t), :], qs, (((1,), (1,)), ((), ())),
                                          preferred_element_type=F32)
                    s_chunks.append(s_t)
                    m_t = jnp.max(s_t, axis=0, keepdims=True)
                    m_acc = m_t if m_acc is None else jnp.maximum(m_acc, m_t)
                if scored is not None:
                    p_t = jnp.exp2(scored[0][t] - scored[1])
                    p_chunks.append(p_t)
                    l_t = jnp.sum(p_t, axis=0, keepdims=True)
                    l_acc = l_t if l_acc is None else l_acc + l_t
                if exped is not None:
                    a_t = (exped[0][t][:, 0:tq] - exped[0][t][:, tq:2 * tq] * ratio).astype(BF16)
                    part = lax.dot_general(v_ref[0, i - 2, keys(t), :], a_t, (((0,), (0,)), ((), ())),
                                           preferred_element_type=F32)
                    o_t = part if o_t is None else o_t + part
            if exped is not None:
                o_t = o_t * (1.0 / l0)
                o_t = o_t * lax.rsqrt(jnp.mean(o_t * o_t, axis=0, keepdims=True) + EPS)
                o = o_t.T * sub_ref[...] * out_scale
                g = g_ref[0, i - 2].astype(F32)
                outs.append((o * _silu(g)).astype(BF16))
            exped = (p_chunks, l_acc) if scored is not None else None
            scored = (s_chunks, m_acc) if qs is not None else None
        o_ref[0] = jnp.concatenate(outs, axis=1)

    if skip_ctx:
        attend(k_ref.shape[2])
        return
    j = pl.program_id(2)

    @pl.when(j == 0)
    def _():
        attend(n_ctx)

    @pl.when(j > 0)
    def _():
        attend(k_ref.shape[2])


def _diff_attention(proj, lam, subln, *, n_ctx, out_scale, skip_ctx):
    b, _, l, _ = proj.shape
    hp = ATTN_HEADS_PER_STEP
    ng = DIFF_HEADS // hp
    skip = 1 if skip_ctx else 0
    kern = functools.partial(_attn_kernel, n_ctx=n_ctx, out_scale=out_scale, skip_ctx=skip_ctx)
    return pl.pallas_call(
        kern,
        out_shape=jax.ShapeDtypeStruct((b, l - skip * TM, DIFF_HEADS * LANES), BF16),
        grid=(b, ng, l // TM - skip),
        in_specs=[
            pl.BlockSpec((1, LANES), lambda i, hh, j: (0, 0)),
            pl.BlockSpec((1, LANES), lambda i, hh, j: (0, 0)),
            pl.BlockSpec((1, hp, TM, LANES), lambda i, hh, j: (i, hh, j + skip, 0)),
            pl.BlockSpec((1, hp, l, LANES), lambda i, hh, j: (i, ng + hh, 0, 0)),
            pl.BlockSpec((1, hp, l, LANES), lambda i, hh, j: (i, 2 * ng + hh, 0, 0)),
            pl.BlockSpec((1, hp, TM, LANES), lambda i, hh, j: (i, 3 * ng + hh, j + skip, 0)),
        ],
        out_specs=pl.BlockSpec((1, TM, hp * LANES), lambda i, hh, j: (i, j, hh)),
        compiler_params=_cparams(3),
        name="diff_attention",
    )(lam, subln.reshape(1, LANES), proj, proj, proj, proj)


def _rope_tables(n_ctx, seq, axis_dim):
    t = jnp.arange(seq)
    rows = (t // GRID_W).astype(F32)
    cols = (t % GRID_W).astype(F32)
    freqs = ROPE_BASE ** (-jnp.arange(axis_dim // 2, dtype=F32) * 2.0 / axis_dim)
    lane = jnp.arange(LANES)
    use_cols = (lane % (2 * axis_dim)) // axis_dim == 1
    within = lane % axis_dim
    first = within < axis_dim // 2
    f = freqs[within % (axis_dim // 2)]
    pos = jnp.where(use_cols[None, :], cols[:, None], rows[:, None])
    ang = pos * f[None, :]
    cos, sin = jnp.cos(ang), jnp.sin(ang)
    sin_up = jnp.where(first[None, :], -sin, 0.0)
    sin_dn = jnp.where(first[None, :], 0.0, sin)
    pad = lambda a, fill: jnp.concatenate([jnp.full((n_ctx, LANES), fill, F32), a.astype(F32)], axis=0)
    return pad(cos, 1.0), pad(sin_up, 0.0), pad(sin_dn, 0.0)


def kernel(x, c, ctx, c_ctx, ada_w, ada_b, norm_pre, norm_post, ev_w_in, ev_w_out, ret_decay_logit, pool_w,
           pool_scale, od_w_in, od_w_out, diff_lambda, diff_subln):
    b, seq, d = x.shape
    n_ctx = ctx.shape[1]
    assert n_ctx == TM and seq % TM == 0 and (n_ctx + seq) % OUT_TM == 0 and d == D_MODEL

    xx = x

    mod_rows = ((b + 1 + 7) // 8) * 8
    c_aug = jnp.zeros((mod_rows, d), F32).at[:b].set(c).at[b].set(c_ctx)
    mod = _modulation(c_aug, ada_w, ada_b)

    lam_inits = tuple(0.8 - 0.6 * math.exp(-0.3 * l) for l in range(1, DEPTH, 2))
    lams = _diff_lambdas(diff_lambda, lam_inits)

    even_tabs = _rope_tables(n_ctx, seq, RET_DK // 2)
    odd_tabs = _rope_tables(n_ctx, seq, DIFF_DH // 2)

    for l in range(DEPTH):
        last = l == DEPTH - 1
        i = l // 2
        m = mod[l].reshape(mod_rows, 3, d)
        modsel = jnp.stack([jnp.broadcast_to(m[b], (b, 3, d)), m[:b]], axis=1)
        g_pre = norm_pre[l].reshape(1, d)
        g_post = norm_post[l].reshape(1, d)
        if l % 2 == 0:
            proj = _project(xx, modsel, g_pre, ev_w_in[i].astype(BF16), even_tabs, ctx=ctx if l == 0 else None,
                            n_rope=2 * RET_HEADS, n_q=RET_HEADS, rot=RET_DK // 4, k_scale=RET_DK ** -0.5)
            if l == 0:
                proj, xx = proj
            dl = jnp.broadcast_to(ret_decay_logit[i].T[:, :, None], (RET_HEADS, 2, LANES))
            z_ret = _retention(proj, dl, n_ctx)
            z_pool = _pooling(proj, pool_w[i].astype(BF16), pool_scale[i], ((0, n_ctx), (n_ctx, seq)))
            xx = _out_project(z_ret, z_pool, 0, ev_w_out[i].astype(BF16), xx, modsel, g_post, n_ctx=n_ctx,
                              latent_only=last)
        else:
            proj = _project(xx, modsel, g_pre, od_w_in[i].astype(BF16), odd_tabs,
                            n_rope=2 * DIFF_HEADS, n_q=2 * DIFF_HEADS, rot=DIFF_DH // 4, k_scale=1.0)
            z = _diff_attention(proj, lams[i], diff_subln[i], n_ctx=n_ctx, out_scale=1.0 - lam_inits[i],
                                skip_ctx=last)
            xx = _out_project(z, z, 1, od_w_out[i].astype(BF16), xx, modsel, g_post, n_ctx=n_ctx, latent_only=last)
    return xx
```
